```python
import math
import jax, jax.numpy as jnp
from jax import lax
import numpy as np

D_MODEL = 1024
BATCH = 8
SEQ = 4096
DEPTH = 2
DEC_BATCH = 128
DEC_SEQ = 1
PAST_LEN = 16384
PAGE_SIZE = 128

HEAD_DIM = 64
H_MLA = 4
Q_LORA = 256
KV_LORA = 128
DN_MLA = 64
DR_MLA = 32
DV_MLA = 64
ROPE_THETA = 10000.0
H_SB = 4
H_NSA = 4
CMP_STRIDE = 16
CMP_BLOCK = 2 * CMP_STRIDE
CMP_HIDDEN = 128
SLC_BLOCK = 64
N_SELECT = 16
N_LOCAL = 2
WINDOW = 512
H_DSA = 4
H_IDX = 4
D_IDX = 32
TOPK_KEYS = 256
NUM_BUCKETS = 32
T5_MAX_DIST = 128
N_EXPERTS = 32
TOP_K = 4
D_FF = 512
SWIGLU_LIMIT = 7.0
SWIGLU_ALPHA = 1.702
MOE_BLOCK = 128
QBLOCK = 128
N_BRANCH = 4
BR_WIDTH = 256
EPS = 1e-6
MLA_KV_W = KV_LORA + DR_MLA
SB_KV_W = 2 * HEAD_DIM
NSA_KV_W = 4 * HEAD_DIM
WIN_KV_W = 2 * HEAD_DIM
DSA_KV_W = 2 * HEAD_DIM + D_IDX
IN_SIZES = (Q_LORA, MLA_KV_W, H_SB * HEAD_DIM, SB_KV_W, H_NSA * HEAD_DIM, NSA_KV_W + WIN_KV_W, 3 * H_NSA,
            H_DSA * HEAD_DIM, 2 * HEAD_DIM, H_IDX * D_IDX, D_IDX, H_IDX, N_BRANCH * D_MODEL)
IN_COLS = (Q_LORA + MLA_KV_W + H_SB * HEAD_DIM + SB_KV_W + H_NSA * HEAD_DIM + NSA_KV_W + WIN_KV_W + 3 * H_NSA
           + H_DSA * HEAD_DIM + 2 * HEAD_DIM + H_IDX * D_IDX + D_IDX + H_IDX + N_BRANCH * D_MODEL)

kernel_name = 'hybrid_mla_sb_nsa_dsa_moe_step'


def rmsnorm(x, g):
    xf = x.astype(jnp.float32)
    y = xf * lax.rsqrt(jnp.mean(xf * xf, axis=-1, keepdims=True) + EPS)
    return y.astype(x.dtype) * g


def adaln(x, g, shift, scale):
    return rmsnorm(x, g) * (1.0 + scale) + shift


def modulation(c, w, b):
    m = jax.nn.silu(c) @ w + b
    return jnp.split(m[:, None, :], 6, axis=-1)


def rope(x, pos):
    half = x.shape[-1] // 2
    freqs = ROPE_THETA ** (-jnp.arange(half, dtype=jnp.float32) / half)
    ang = pos.astype(jnp.float32)[..., None] * freqs
    cos, sin = jnp.cos(ang), jnp.sin(ang)
    xf = x.astype(jnp.float32)
    x1, x2 = xf[..., :half], xf[..., half:]
    return jnp.concatenate([x1 * cos - x2 * sin, x2 * cos + x1 * sin], axis=-1).astype(x.dtype)


def t5_bias(table, dist):
    n = jnp.maximum(dist, 0)
    exact = NUM_BUCKETS // 2
    big = exact + (jnp.log(jnp.maximum(n, exact).astype(jnp.float32) / exact)
                   / math.log(T5_MAX_DIST / exact) * (NUM_BUCKETS - exact)).astype(jnp.int32)
    bucket = jnp.where(n < exact, n, jnp.minimum(big, NUM_BUCKETS - 1))
    return jnp.moveaxis(table[bucket].astype(jnp.float32), -1, 0)


def masked_softmax(s, mask):
    s = jnp.where(mask, s, -jnp.inf)
    m = jnp.max(s, axis=-1, keepdims=True)
    e = jnp.exp(s - jnp.where(jnp.isfinite(m), m, 0.0))
    d = jnp.sum(e, axis=-1, keepdims=True)
    return e / jnp.where(d > 0, d, 1.0)


def sweep(fn, P, *qarrs):
    lq = qarrs[0].shape[0]
    qb = QBLOCK if lq % QBLOCK == 0 else lq
    qpos = P + jnp.arange(lq, dtype=jnp.int32)

    def body(b):
        start = b * qb
        blocks = [lax.dynamic_slice_in_dim(a, start, qb, 0) for a in qarrs]
        return fn(start, lax.dynamic_slice_in_dim(qpos, start, qb, 0), *blocks)

    out = lax.map(body, jnp.arange(lq // qb, dtype=jnp.int32))
    return out.reshape((lq,) + out.shape[2:])


def project(h, pos, p):
    B, L, _ = h.shape
    hd = HEAD_DIM
    cuts = np.cumsum(IN_SIZES)[:-1].tolist()
    mqa, mkv, sq, skv, nq, nkv, ng, dq, dkv, diq, dik, diw, gl = jnp.split(h @ p['w_in'], cuts, axis=-1)
    q = (rmsnorm(mqa, p['g_mla_qa']) @ p['w_mla_uq']).reshape(B, L, H_MLA, DN_MLA + DR_MLA)
    gq = p['g_mla_q']
    mla_q = jnp.concatenate([rmsnorm(q[..., :DN_MLA], gq[:DN_MLA]),
                             rope(rmsnorm(q[..., DN_MLA:], gq[DN_MLA:]), pos[:, None])], axis=-1)
    mla_rows = jnp.concatenate([rmsnorm(mkv[..., :KV_LORA], p['g_mla_kva']),
                                rope(rmsnorm(mkv[..., KV_LORA:], p['g_mla_k'][DN_MLA:]), pos)], axis=-1)
    sb_q = sq.reshape(B, L, H_SB, hd)
    nsa_q = rmsnorm(nq.reshape(B, L, H_NSA, hd), p['g_nsa_q'])
    gk = p['g_nsa_k']
    nsa_rows = jnp.concatenate([nkv[..., :2 * hd], rmsnorm(nkv[..., 2 * hd:3 * hd], gk[1]), nkv[..., 3 * hd:4 * hd]], axis=-1)
    win_rows = jnp.concatenate([rmsnorm(nkv[..., 4 * hd:5 * hd], gk[2]), nkv[..., 5 * hd:]], axis=-1)
    nsa_g = jax.nn.sigmoid(ng.reshape(B, L, H_NSA, 3))
    dsa_q = rmsnorm(dq.reshape(B, L, H_DSA, hd), p['g_dsa_q'])
    dsa_rows = jnp.concatenate([rmsnorm(dkv[..., :hd], p['g_dsa_k']), dkv[..., hd:], dik], axis=-1)
    q_idx = diq.reshape(B, L, H_IDX, D_IDX)
    gates = jax.nn.sigmoid(gl.reshape(B, L, N_BRANCH, D_MODEL))
    qs = (mla_q, sb_q, nsa_q, nsa_g, dsa_q, q_idx, diw)
    rows = (mla_rows, sb_rows_fn(skv), nsa_rows, win_rows, dsa_rows)
    return qs, rows, gates


def sb_rows_fn(skv):
    return skv


def mla_attend(q, rows, P, w_ukv, g_kn):
    lk = rows.shape[0]
    kv = (rows[:, :KV_LORA] @ w_ukv).reshape(lk, H_MLA, DN_MLA + DV_MLA)
    k_n = rmsnorm(kv[..., :DN_MLA], g_kn)
    v = kv[..., DN_MLA:]
    k_r = rows[:, KV_LORA:]
    kpos = jnp.arange(lk, dtype=jnp.int32)
    scale = (DN_MLA + DR_MLA) ** -0.5

    def blk(start, qpos, qb):
        s = (jnp.einsum('qhd,khd->hqk', qb[..., :DN_MLA], k_n).astype(jnp.float32)
             + jnp.einsum('qhd,kd->hqk', qb[..., DN_MLA:], k_r).astype(jnp.float32)) * scale
        pr = masked_softmax(s, kpos[None, :] <= qpos[:, None])
        return jnp.einsum('hqk,khd->qhd', pr.astype(v.dtype), v).reshape(qb.shape[0], H_MLA * DV_MLA)

    return sweep(blk, P, q)


def sb_attend(q, rows, P):
    lk = rows.shape[0]
    k, v = rows[:, :HEAD_DIM], rows[:, HEAD_DIM:]
    kpos = jnp.arange(lk, dtype=jnp.int32)

    def blk(start, qpos, qb):
        z = jnp.einsum('qhd,kd->hqk', qb, k).astype(jnp.float32) * HEAD_DIM ** -0.5
        before = kpos[None, :] < qpos[:, None]
        log_beta = jnp.where(before, jax.nn.log_sigmoid(z), -jnp.inf)
        log_keep = jnp.where(before, jax.nn.log_sigmoid(-z), 0.0)
        later = lax.cumsum(log_keep, axis=2, reverse=True) - log_keep
        a = jnp.exp(log_beta + later)
        return jnp.einsum('hqk,kd->qhd', a.astype(v.dtype), v).reshape(qb.shape[0], H_SB * HEAD_DIM)

    return sweep(blk, P, q)


def nsa_attend(q, gate, nsa_rows, win_rows, P, w_len, p, table):
    hd = HEAD_DIM
    lk = nsa_rows.shape[0]
    scale = hd ** -0.5
    n_chunk = lk // CMP_STRIDE
    n_cmp = n_chunk - 1

    def compress(x, i):
        ch = x[: n_chunk * CMP_STRIDE].reshape(n_chunk, CMP_STRIDE, hd)
        blocks = jnp.concatenate([ch[:-1], ch[1:]], axis=1) + p['pe_cmp'][i]
        return jax.nn.gelu(blocks.reshape(n_cmp, CMP_BLOCK * hd) @ p['w_cmp1'][i]) @ p['w_cmp2'][i]

    k_cmp = rmsnorm(compress(nsa_rows[:, :hd], 0), p['g_nsa_k'][0])
    v_cmp = compress(nsa_rows[:, hd:2 * hd], 1)
    cmp_end = jnp.arange(n_cmp, dtype=jnp.int32) * CMP_STRIDE + (CMP_BLOCK - 1)
    n_slc = -(-lk // SLC_BLOCK)
    n_sel = min(N_SELECT, n_slc)
    slc = jnp.pad(nsa_rows[:, 2 * hd:], ((0, n_slc * SLC_BLOCK - lk), (0, 0))).reshape(n_slc, SLC_BLOCK, 2 * hd)
    ratio = SLC_BLOCK // CMP_STRIDE
    span = CMP_BLOCK // CMP_STRIDE
    blk_ids = jnp.arange(n_slc, dtype=jnp.int32)
    lw = win_rows.shape[0]
    wpos = P - WINDOW + jnp.arange(lw, dtype=jnp.int32)
    wvalid = wpos >= P - w_len

    def blk(start, qpos, qb, gb):
        nq = qb.shape[0]
        qf = qpos[:, None]
        dist_c = qf - cmp_end[None, :]
        s = jnp.einsum('qhd,kd->hqk', qb, k_cmp).astype(jnp.float32) * scale + t5_bias(table, dist_c)
        pc = masked_softmax(s, dist_c >= 0)
        o_c = jnp.einsum('hqk,kd->qhd', pc.astype(v_cmp.dtype), v_cmp)
        imp = jnp.pad(pc.sum(0), ((0, 0), (span - 1, ratio * (n_slc + 1) - (span - 1) - n_cmp)))
        score = jnp.zeros((nq, n_slc), jnp.float32)
        for m in range(ratio):
            for n in range(span):
                score = score + imp[:, m + n: m + n + ratio * n_slc: ratio]
        back = qf // SLC_BLOCK - blk_ids[None, :]
        forced = (blk_ids[None, :] == 0) | ((back >= 0) & (back < N_LOCAL))
        score = jnp.where(back >= 0, jnp.where(forced, jnp.inf, score), -jnp.inf)
        top_s, top_i = lax.top_k(score, n_sel)
        g = slc[top_i].reshape(nq, n_sel * SLC_BLOCK, 2 * hd)
        kpos = (top_i[..., None] * SLC_BLOCK + jnp.arange(SLC_BLOCK, dtype=jnp.int32)).reshape(nq, -1)
        ok = jnp.repeat(top_s > -jnp.inf, SLC_BLOCK, axis=1) & (kpos <= qf)
        s = jnp.einsum('qhd,qkd->hqk', qb, g[..., :hd]).astype(jnp.float32) * scale + t5_bias(table, qf - kpos)
        ps = masked_softmax(s, ok[None])
        o_s = jnp.einsum('hqk,qkd->qhd', ps.astype(g.dtype), g[..., hd:])
        wk = lax.dynamic_slice_in_dim(win_rows, start, nq + WINDOW, 0)
        wp = lax.dynamic_slice_in_dim(wpos, start, nq + WINDOW, 0)
        wv = lax.dynamic_slice_in_dim(wvalid, start, nq + WINDOW, 0)
        dist_w = qf - wp[None, :]
        s = jnp.einsum('qhd,kd->hqk', qb, wk[:, :hd]).astype(jnp.float32) * scale + t5_bias(table, dist_w)
        pw = masked_softmax(s, (dist_w >= 0) & (dist_w < WINDOW) & wv[None, :])
        o_w = jnp.einsum('hqk,kd->qhd', pw.astype(wk.dtype), wk[:, hd:])
        o = gb[..., 0:1] * o_c + gb[..., 1:2] * o_s + gb[..., 2:3] * o_w
        return o.reshape(nq, H_NSA * hd)

    return sweep(blk, P, q, gate)


def dsa_attend(q, q_idx, w_idx, rows, P, table):
    hd = HEAD_DIM
    lk = rows.shape[0]
    k, v, k_idx = rows[:, :hd], rows[:, hd:2 * hd], rows[:, 2 * hd:]
    n_keep = min(TOPK_KEYS, lk // 4)
    kpos = jnp.arange(lk, dtype=jnp.int32)
    scale = hd ** -0.5

    def blk(start, qpos, qb, qib, wb):
        qf = qpos[:, None]
        rel = jax.nn.relu(jnp.einsum('qhd,kd->qhk', qib, k_idx).astype(jnp.float32))
        isc = jnp.einsum('qh,qhk->qk', wb.astype(jnp.float32), rel)
        isc = jnp.where(kpos[None, :] <= qf, isc, -jnp.inf)
        top_s, top_i = lax.top_k(isc, n_keep)
        s = jnp.einsum('qhd,qkd->hqk', qb, k[top_i]).astype(jnp.float32) * scale + t5_bias(table, qf - top_i)
        pr = masked_softmax(s, (top_s > -jnp.inf)[None])
        return jnp.einsum('hqk,qkd->qhd', pr.astype(v.dtype), v[top_i]).reshape(qb.shape[0], H_DSA * hd)

    return sweep(blk, P, q, q_idx, w_idx)


def attend_seq(qs, rows, past, P, w_len, p, table):
    mla_q, sb_q, nsa_q, nsa_g, dsa_q, q_idx, w_idx = qs
    mla_r, sb_r, nsa_r, win_r, dsa_r = [jnp.concatenate([a, b], axis=0) for a, b in zip(past, rows)]
    win_r = jnp.pad(win_r, ((WINDOW - w_len, 0), (0, 0)))
    o_mla = mla_attend(mla_q, mla_r, P, p['w_mla_ukv'], p['g_mla_k'][:DN_MLA])
    o_sb = sb_attend(sb_q, sb_r, P)
    o_nsa = nsa_attend(nsa_q, nsa_g, nsa_r, win_r, P, w_len, p, table[:, :H_NSA])
    o_dsa = dsa_attend(dsa_q, q_idx, w_idx, dsa_r, P, table[:, H_NSA:])
    return jnp.stack([o_mla, o_sb, o_nsa, o_dsa], axis=1)


def merge(o, gates, w_br, w_out):
    br = jnp.einsum('blmc,mcd->blmd', o, w_br)
    return jnp.sum(gates * br, axis=2) @ w_out


def moe_ffn(h, w_router, b_router, w_gu, b_gu, w_down, b_down):
    shp = h.shape
    xf = h.reshape(-1, shp[-1])
    t = xf.shape[0]
    logits = (xf @ w_router + b_router).astype(jnp.float32)
    top_v, top_e = lax.top_k(logits, TOP_K)
    wts = jax.nn.softmax(top_v, axis=-1).astype(h.dtype)
    a = t * TOP_K
    flat_e = top_e.reshape(-1)
    order = jnp.argsort(flat_e)
    sorted_e = flat_e[order]
    counts = jnp.zeros((N_EXPERTS,), jnp.int32).at[flat_e].add(1)
    first = jnp.cumsum(counts) - counts
    padded = (counts + MOE_BLOCK - 1) // MOE_BLOCK * MOE_BLOCK
    pad_end = jnp.cumsum(padded)
    pad_start = pad_end - padded
    dest = (pad_start[sorted_e] + jnp.arange(a, dtype=jnp.int32) - first[sorted_e]).astype(jnp.int32)
    nb = -(-a // MOE_BLOCK) + N_EXPERTS
    slot_tok = jnp.full((nb * MOE_BLOCK,), t, jnp.int32).at[dest].set((order // TOP_K).astype(jnp.int32))
    blk_e = jnp.minimum(jnp.searchsorted(pad_end, jnp.arange(nb, dtype=jnp.int32) * MOE_BLOCK, side='right'),
                        N_EXPERTS - 1).astype(jnp.int32)
    xpad = jnp.concatenate([xf, jnp.zeros((1, xf.shape[1]), xf.dtype)], axis=0)

    def body(args):
        b, e = args
        xb = xpad[lax.dynamic_slice_in_dim(slot_tok, b * MOE_BLOCK, MOE_BLOCK, 0)]
        gu = xb @ w_gu[e] + b_gu[e]
        glu = jnp.minimum(gu[:, 0::2], SWIGLU_LIMIT)
        lin = jnp.clip(gu[:, 1::2], -SWIGLU_LIMIT, SWIGLU_LIMIT)
        act = glu * jax.nn.sigmoid(SWIGLU_ALPHA * glu) * (lin + 1.0)
        return act @ w_down[e] + b_down[e]

    out = lax.map(body, (jnp.arange(nb, dtype=jnp.int32), blk_e)).reshape(nb * MOE_BLOCK, -1)
    slot_of = jnp.zeros((a,), jnp.int32).at[order].set(dest)
    y = jnp.einsum('tk,tkd->td', wts, out[slot_of].reshape(t, TOP_K, -1))
    return y.reshape(shp)


def setup_inputs(seed: int = 0) -> dict:
    key = jax.random.key(seed)
    keys = jax.random.split(key, 40)

    def nrm(i, shape, s=1.0):
        return jax.random.normal(keys[i], shape, jnp.float32) * s

    def gain(i, shape):
        return 1.0 + nrm(i, shape, 0.01)

    D = D_MODEL
    hd = HEAD_DIM
    n_pages = PAST_LEN // PAGE_SIZE
    n_pool = (DEC_BATCH * n_pages * 5) // 4
    w_len = min(WINDOW, PAST_LEN)
    page_table = jax.random.permutation(keys[7], n_pool)[: DEC_BATCH * n_pages].reshape(DEC_BATCH, n_pages).astype(jnp.int32)
    return {
        'x_prompt': nrm(0, (BATCH, SEQ, D)),
        'x_sample': nrm(1, (DEC_BATCH, DEC_SEQ, D)),
        'cache_mla': nrm(2, (DEPTH, n_pool, PAGE_SIZE, MLA_KV_W)),
        'cache_sb': nrm(3, (DEPTH, n_pool, PAGE_SIZE, SB_KV_W)),
        'cache_nsa': nrm(4, (DEPTH, n_pool, PAGE_SIZE, NSA_KV_W)),
        'cache_dsa': nrm(5, (DEPTH, n_pool, PAGE_SIZE, DSA_KV_W)),
        'state_nsa_win': nrm(6, (DEPTH, DEC_BATCH, w_len, WIN_KV_W)),
        'page_table': page_table,
        'c_prompt': nrm(8, (BATCH, D)),
        'c_sample': nrm(9, (DEC_BATCH, D)),
        'w_ada': nrm(10, (DEPTH, D, 6 * D), 0.5 * D ** -0.5),
        'b_ada': nrm(11, (DEPTH, 6 * D), 0.01),
        'g_norm_mix': gain(12, (DEPTH, D)),
        'g_norm_ffn': gain(13, (DEPTH, D)),
        'w_in': nrm(14, (DEPTH, D, IN_COLS), D ** -0.5),
        'g_mla_qa': gain(15, (DEPTH, Q_LORA)),
        'w_mla_uq': nrm(16, (DEPTH, Q_LORA, H_MLA * (DN_MLA + DR_MLA)), Q_LORA ** -0.5),
        'g_mla_kva': gain(17, (DEPTH, KV_LORA)),
        'w_mla_ukv': nrm(18, (DEPTH, KV_LORA, H_MLA * (DN_MLA + DV_MLA)), KV_LORA ** -0.5),
        'g_mla_q': gain(19, (DEPTH, DN_MLA + DR_MLA)),
        'g_mla_k': gain(20, (DEPTH, DN_MLA + DR_MLA)),
        'w_cmp1': nrm(21, (DEPTH, 2, CMP_BLOCK * hd, CMP_HIDDEN), (CMP_BLOCK * hd) ** -0.5),
        'w_cmp2': nrm(22, (DEPTH, 2, CMP_HIDDEN, hd), CMP_HIDDEN ** -0.5),
        'pe_cmp': nrm(23, (DEPTH, 2, CMP_BLOCK, hd), 0.1),
        'g_nsa_q': gain(24, (DEPTH, hd)),
        'g_nsa_k': gain(25, (DEPTH, 3, hd)),
        'g_dsa_q': gain(26, (DEPTH, hd)),
        'g_dsa_k': gain(27, (DEPTH, hd)),
        'rel_bias': nrm(28, (NUM_BUCKETS, H_NSA + H_DSA), 0.5),
        'w_br': nrm(29, (DEPTH, N_BRANCH, BR_WIDTH, D), BR_WIDTH ** -0.5),
        'w_out': nrm(30, (DEPTH, D, D), D ** -0.5),
        'w_router': nrm(31, (DEPTH, D, N_EXPERTS), D ** -0.5),
        'b_router': nrm(32, (DEPTH, N_EXPERTS), 0.01),
        'w_gu': nrm(33, (DEPTH, N_EXPERTS, D, 2 * D_FF), D ** -0.5),
        'b_gu': nrm(34, (DEPTH, N_EXPERTS, 2 * D_FF), 0.01),
        'w_down': nrm(35, (DEPTH, N_EXPERTS, D_FF, D), D_FF ** -0.5),
        'b_down': nrm(36, (DEPTH, N_EXPERTS, D), 0.01),
    }


def reference(x_prompt, x_sample, cache_mla, cache_sb, cache_nsa, cache_dsa, state_nsa_win, page_table,
              c_prompt, c_sample, w_ada, b_ada, g_norm_mix, g_norm_ffn, w_in, g_mla_qa, w_mla_uq, g_mla_kva,
              w_mla_ukv, g_mla_q, g_mla_k, w_cmp1, w_cmp2, pe_cmp, g_nsa_q, g_nsa_k, g_dsa_q, g_dsa_k, rel_bias,
              w_br, w_out, w_router, b_router, w_gu, b_gu, w_down, b_down):
    seq_p = x_prompt.shape[1]
    seq_s = x_sample.shape[1]
    w_len = state_nsa_win.shape[2]
    w_len_p = min(WINDOW, seq_p)
    pos_p = jnp.arange(seq_p, dtype=jnp.int32)
    pos_s = PAST_LEN + jnp.arange(seq_s, dtype=jnp.int32)
    xp, xs = x_prompt, x_sample
    out_p = []
    out_s = []
    for l in range(DEPTH):
        p = {'w_in': w_in[l], 'g_mla_qa': g_mla_qa[l], 'w_mla_uq': w_mla_uq[l], 'g_mla_kva': g_mla_kva[l],
             'w_mla_ukv': w_mla_ukv[l], 'g_mla_q': g_mla_q[l], 'g_mla_k': g_mla_k[l], 'w_cmp1': w_cmp1[l],
             'w_cmp2': w_cmp2[l], 'pe_cmp': pe_cmp[l], 'g_nsa_q': g_nsa_q[l], 'g_nsa_k': g_nsa_k[l],
             'g_dsa_q': g_dsa_q[l], 'g_dsa_k': g_dsa_k[l]}

        def prompt_one(qs, rows):
            past = tuple(jnp.zeros((0, r.shape[-1]), r.dtype) for r in rows)
            return attend_seq(qs, rows, past, 0, 0, p, rel_bias)

        def sample_one(args):
            qs, rows, prow, wbuf = args
            past = (cache_mla[l, prow].reshape(-1, MLA_KV_W), cache_sb[l, prow].reshape(-1, SB_KV_W),
                    cache_nsa[l, prow].reshape(-1, NSA_KV_W), wbuf, cache_dsa[l, prow].reshape(-1, DSA_KV_W))
            return attend_seq(qs, rows, past, PAST_LEN, w_len, p, rel_bias)

        mp = modulation(c_prompt, w_ada[l], b_ada[l])
        ms = modulation(c_sample, w_ada[l], b_ada[l])
        qs_p, rows_p, gates_p = project(adaln(xp, g_norm_mix[l], mp[0], mp[1]), pos_p, p)
        o_p = jax.vmap(prompt_one)(qs_p, rows_p)
        xp = xp + mp[2] * merge(o_p, gates_p, w_br[l], w_out[l])
        qs_s, rows_s, gates_s = project(adaln(xs, g_norm_mix[l], ms[0], ms[1]), pos_s, p)
        o_s = lax.map(sample_one, (qs_s, rows_s, page_table, state_nsa_win[l]))
        xs = xs + ms[2] * merge(o_s, gates_s, w_br[l], w_out[l])
        xp = xp + mp[5] * moe_ffn(adaln(xp, g_norm_ffn[l], mp[3], mp[4]), w_router[l], b_router[l],
                                  w_gu[l], b_gu[l], w_down[l], b_down[l])
        xs = xs + ms[5] * moe_ffn(adaln(xs, g_norm_ffn[l], ms[3], ms[4]), w_router[l], b_router[l],
                                  w_gu[l], b_gu[l], w_down[l], b_down[l])
        out_p.append((rows_p[0], rows_p[1], rows_p[2], rows_p[4], rows_p[3][:, seq_p - w_len_p:]))
        win_s = jnp.concatenate([state_nsa_win[l], rows_s[3]], axis=1)
        out_s.append((rows_s[0], rows_s[1], rows_s[2], rows_s[4], win_s[:, win_s.shape[1] - w_len:]))
    new_mla_p, new_sb_p, new_nsa_p, new_dsa_p, new_win_p = [jnp.stack([o[i] for o in out_p], axis=0) for i in range(5)]
    new_mla_s, new_sb_s, new_nsa_s, new_dsa_s, new_win_s = [jnp.stack([o[i] for o in out_s], axis=0) for i in range(5)]
    return (xp, xs, new_mla_p, new_sb_p, new_nsa_p, new_dsa_p, new_win_p,
            new_mla_s, new_sb_s, new_nsa_s, new_dsa_s, new_win_s)
```

```python
import math
import jax, jax.numpy as jnp
from jax import lax
import numpy as np
from jax.experimental import pallas as pl
from jax.experimental.pallas import tpu as pltpu

D_MODEL = 1024
DEPTH = 2
PAST_LEN = 16384
PAGE_SIZE = 128
HEAD_DIM = 64
H_MLA = 4
Q_LORA = 256
KV_LORA = 128
DN_MLA = 64
DR_MLA = 32
DV_MLA = 64
ROPE_THETA = 10000.0
H_SB = 4
H_NSA = 4
CMP_STRIDE = 16
CMP_BLOCK = 2 * CMP_STRIDE
CMP_HIDDEN = 128
SLC_BLOCK = 64
N_SELECT = 16
N_LOCAL = 2
WINDOW = 512
H_DSA = 4
H_IDX = 4
D_IDX = 32
TOPK_KEYS = 256
NUM_BUCKETS = 32
T5_MAX_DIST = 128
N_EXPERTS = 32
TOP_K = 4
D_FF = 512
SWIGLU_LIMIT = 7.0
SWIGLU_ALPHA = 1.702
MOE_BLOCK = 128
QBLOCK = 128
N_BRANCH = 4
BR_WIDTH = 256
EPS = 1e-6
MLA_KV_W = KV_LORA + DR_MLA
SB_KV_W = 2 * HEAD_DIM
NSA_KV_W = 4 * HEAD_DIM
WIN_KV_W = 2 * HEAD_DIM
DSA_KV_W = 2 * HEAD_DIM + D_IDX
IN_SIZES = (Q_LORA, MLA_KV_W, H_SB * HEAD_DIM, SB_KV_W, H_NSA * HEAD_DIM, NSA_KV_W + WIN_KV_W, 3 * H_NSA,
            H_DSA * HEAD_DIM, 2 * HEAD_DIM, H_IDX * D_IDX, D_IDX, H_IDX, N_BRANCH * D_MODEL)


def _mm_kernel(x_ref, w_ref, o_ref):
    o_ref[...] = jnp.dot(x_ref[...].astype(jnp.bfloat16), w_ref[...], preferred_element_type=jnp.float32)


def pallas_mm(x, w, tm=512, tn=1024):
    m, k = x.shape
    n = w.shape[1]
    np_ = -(-n // tn) * tn
    mp = -(-m // tm) * tm if m > tm else m
    tm = min(tm, mp)
    wb = jnp.pad(w.astype(jnp.bfloat16), ((0, 0), (0, np_ - n)))
    xp = jnp.pad(x, ((0, mp - m), (0, 0)))
    out = pl.pallas_call(
        _mm_kernel,
        grid=(mp // tm, np_ // tn),
        in_specs=[pl.BlockSpec((tm, k), lambda i, j: (i, 0)), pl.BlockSpec((k, tn), lambda i, j: (0, j))],
        out_specs=pl.BlockSpec((tm, tn), lambda i, j: (i, j)),
        out_shape=jax.ShapeDtypeStruct((mp, np_), jnp.float32),
    )(xp, wb)
    return out[:m, :n]


def rmsnorm(x, g):
    xf = x.astype(jnp.float32)
    y = xf * lax.rsqrt(jnp.mean(xf * xf, axis=-1, keepdims=True) + EPS)
    return y.astype(x.dtype) * g


def adaln(x, g, shift, scale):
    return rmsnorm(x, g) * (1.0 + scale) + shift


def modulation(c, w, b):
    m = jax.nn.silu(c) @ w + b
    return jnp.split(m[:, None, :], 6, axis=-1)


def rope(x, pos):
    half = x.shape[-1] // 2
    freqs = ROPE_THETA ** (-jnp.arange(half, dtype=jnp.float32) / half)
    ang = pos.astype(jnp.float32)[..., None] * freqs
    cos, sin = jnp.cos(ang), jnp.sin(ang)
    xf = x.astype(jnp.float32)
    x1, x2 = xf[..., :half], xf[..., half:]
    return jnp.concatenate([x1 * cos - x2 * sin, x2 * cos + x1 * sin], axis=-1).astype(x.dtype)


def t5_bias(table, dist):
    n = jnp.maximum(dist, 0)
    exact = NUM_BUCKETS // 2
    big = exact + (jnp.log(jnp.maximum(n, exact).astype(jnp.float32) / exact)
                   / math.log(T5_MAX_DIST / exact) * (NUM_BUCKETS - exact)).astype(jnp.int32)
    bucket = jnp.where(n < exact, n, jnp.minimum(big, NUM_BUCKETS - 1))
    return jnp.moveaxis(table[bucket].astype(jnp.float32), -1, 0)


def masked_softmax(s, mask):
    s = jnp.where(mask, s, -jnp.inf)
    m = jnp.max(s, axis=-1, keepdims=True)
    e = jnp.exp(s - jnp.where(jnp.isfinite(m), m, 0.0))
    d = jnp.sum(e, axis=-1, keepdims=True)
    return e / jnp.where(d > 0, d, 1.0)


def sweep(fn, P, *qarrs):
    lq = qarrs[0].shape[0]
    qb = QBLOCK if lq % QBLOCK == 0 else lq
    qpos = P + jnp.arange(lq, dtype=jnp.int32)

    def body(b):
        start = b * qb
        blocks = [lax.dynamic_slice_in_dim(a, start, qb, 0) for a in qarrs]
        return fn(start, lax.dynamic_slice_in_dim(qpos, start, qb, 0), *blocks)

    out = lax.map(body, jnp.arange(lq // qb, dtype=jnp.int32))
    return out.reshape((lq,) + out.shape[2:])


def project(h, pos, p):
    B, L, _ = h.shape
    hd = HEAD_DIM
    cuts = np.cumsum(IN_SIZES)[:-1].tolist()
    y = pallas_mm(h.reshape(B * L, -1), p['w_in']).reshape(B, L, -1)
    mqa, mkv, sq, skv, nq, nkv, ng, dq, dkv, diq, dik, diw, gl = jnp.split(y, cuts, axis=-1)
    q = (rmsnorm(mqa, p['g_mla_qa']) @ p['w_mla_uq']).reshape(B, L, H_MLA, DN_MLA + DR_MLA)
    gq = p['g_mla_q']
    mla_q = jnp.concatenate([rmsnorm(q[..., :DN_MLA], gq[:DN_MLA]),
                             rope(rmsnorm(q[..., DN_MLA:], gq[DN_MLA:]), pos[:, None])], axis=-1)
    mla_rows = jnp.concatenate([rmsnorm(mkv[..., :KV_LORA], p['g_mla_kva']),
                                rope(rmsnorm(mkv[..., KV_LORA:], p['g_mla_k'][DN_MLA:]), pos)], axis=-1)
    sb_q = sq.reshape(B, L, H_SB, hd)
    nsa_q = rmsnorm(nq.reshape(B, L, H_NSA, hd), p['g_nsa_q'])
    gk = p['g_nsa_k']
    nsa_rows = jnp.concatenate([nkv[..., :2 * hd], rmsnorm(nkv[..., 2 * hd:3 * hd], gk[1]), nkv[..., 3 * hd:4 * hd]], axis=-1)
    win_rows = jnp.concatenate([rmsnorm(nkv[..., 4 * hd:5 * hd], gk[2]), nkv[..., 5 * hd:]], axis=-1)
    nsa_g = jax.nn.sigmoid(ng.reshape(B, L, H_NSA, 3))
    dsa_q = rmsnorm(dq.reshape(B, L, H_DSA, hd), p['g_dsa_q'])
    dsa_rows = jnp.concatenate([rmsnorm(dkv[..., :hd], p['g_dsa_k']), dkv[..., hd:], dik], axis=-1)
    q_idx = diq.reshape(B, L, H_IDX, D_IDX)
    gates = jax.nn.sigmoid(gl.reshape(B, L, N_BRANCH, D_MODEL))
    qs = (mla_q, sb_q, nsa_q, nsa_g, dsa_q, q_idx, diw)
    rows = (mla_rows, skv, nsa_rows, win_rows, dsa_rows)
    return qs, rows, gates


def mla_attend(q, rows, P, w_ukv, g_kn):
    lk = rows.shape[0]
    kv = (rows[:, :KV_LORA] @ w_ukv).reshape(lk, H_MLA, DN_MLA + DV_MLA)
    k_n = rmsnorm(kv[..., :DN_MLA], g_kn)
    v = kv[..., DN_MLA:]
    k_r = rows[:, KV_LORA:]
    kpos = jnp.arange(lk, dtype=jnp.int32)
    scale = (DN_MLA + DR_MLA) ** -0.5

    def blk(start, qpos, qb):
        s = (jnp.einsum('qhd,khd->hqk', qb[..., :DN_MLA], k_n).astype(jnp.float32)
             + jnp.einsum('qhd,kd->hqk', qb[..., DN_MLA:], k_r).astype(jnp.float32)) * scale
        pr = masked_softmax(s, kpos[None, :] <= qpos[:, None])
        return jnp.einsum('hqk,khd->qhd', pr.astype(v.dtype), v).reshape(qb.shape[0], H_MLA * DV_MLA)

    return sweep(blk, P, q)


def sb_attend(q, rows, P):
    lk = rows.shape[0]
    k, v = rows[:, :HEAD_DIM], rows[:, HEAD_DIM:]
    kpos = jnp.arange(lk, dtype=jnp.int32)

    def blk(start, qpos, qb):
        z = jnp.einsum('qhd,kd->hqk', qb, k).astype(jnp.float32) * HEAD_DIM ** -0.5
        before = kpos[None, :] < qpos[:, None]
        log_beta = jnp.where(before, jax.nn.log_sigmoid(z), -jnp.inf)
        log_keep = jnp.where(before, jax.nn.log_sigmoid(-z), 0.0)
        later = lax.cumsum(log_keep, axis=2, reverse=True) - log_keep
        a = jnp.exp(log_beta + later)
        return jnp.einsum('hqk,kd->qhd', a.astype(v.dtype), v).reshape(qb.shape[0], H_SB * HEAD_DIM)

    return sweep(blk, P, q)


def nsa_attend(q, gate, nsa_rows, win_rows, P, w_len, p, table):
    hd = HEAD_DIM
    lk = nsa_rows.shape[0]
    scale = hd ** -0.5
    n_chunk = lk // CMP_STRIDE
    n_cmp = n_chunk - 1

    def compress(x, i):
        ch = x[: n_chunk * CMP_STRIDE].reshape(n_chunk, CMP_STRIDE, hd)
        blocks = jnp.concatenate([ch[:-1], ch[1:]], axis=1) + p['pe_cmp'][i]
        return jax.nn.gelu(blocks.reshape(n_cmp, CMP_BLOCK * hd) @ p['w_cmp1'][i]) @ p['w_cmp2'][i]

    k_cmp = rmsnorm(compress(nsa_rows[:, :hd], 0), p['g_nsa_k'][0])
    v_cmp = compress(nsa_rows[:, hd:2 * hd], 1)
    cmp_end = jnp.arange(n_cmp, dtype=jnp.int32) * CMP_STRIDE + (CMP_BLOCK - 1)
    n_slc = -(-lk // SLC_BLOCK)
    n_sel = min(N_SELECT, n_slc)
    slc = jnp.pad(nsa_rows[:, 2 * hd:], ((0, n_slc * SLC_BLOCK - lk), (0, 0))).reshape(n_slc, SLC_BLOCK, 2 * hd)
    ratio = SLC_BLOCK // CMP_STRIDE
    span = CMP_BLOCK // CMP_STRIDE
    blk_ids = jnp.arange(n_slc, dtype=jnp.int32)
    lw = win_rows.shape[0]
    wpos = P - WINDOW + jnp.arange(lw, dtype=jnp.int32)
    wvalid = wpos >= P - w_len

    def blk(start, qpos, qb, gb):
        nq = qb.shape[0]
        qf = qpos[:, None]
        dist_c = qf - cmp_end[None, :]
        s = jnp.einsum('qhd,kd->hqk', qb, k_cmp).astype(jnp.float32) * scale + t5_bias(table, dist_c)
        pc = masked_softmax(s, dist_c >= 0)
        o_c = jnp.einsum('hqk,kd->qhd', pc.astype(v_cmp.dtype), v_cmp)
        imp = jnp.pad(pc.sum(0), ((0, 0), (span - 1, ratio * (n_slc + 1) - (span - 1) - n_cmp)))
        score = jnp.zeros((nq, n_slc), jnp.float32)
        for m in range(ratio):
            for n in range(span):
                score = score + imp[:, m + n: m + n + ratio * n_slc: ratio]
        back = qf // SLC_BLOCK - blk_ids[None, :]
        forced = (blk_ids[None, :] == 0) | ((back >= 0) & (back < N_LOCAL))
        score = jnp.where(back >= 0, jnp.where(forced, jnp.inf, score), -jnp.inf)
        top_s, top_i = lax.top_k(score, n_sel)
        g = slc[top_i].reshape(nq, n_sel * SLC_BLOCK, 2 * hd)
        kpos = (top_i[..., None] * SLC_BLOCK + jnp.arange(SLC_BLOCK, dtype=jnp.int32)).reshape(nq, -1)
        ok = jnp.repeat(top_s > -jnp.inf, SLC_BLOCK, axis=1) & (kpos <= qf)
        s = jnp.einsum('qhd,qkd->hqk', qb, g[..., :hd]).astype(jnp.float32) * scale + t5_bias(table, qf - kpos)
        ps = masked_softmax(s, ok[None])
        o_s = jnp.einsum('hqk,qkd->qhd', ps.astype(g.dtype), g[..., hd:])
        wk = lax.dynamic_slice_in_dim(win_rows, start, nq + WINDOW, 0)
        wp = lax.dynamic_slice_in_dim(wpos, start, nq + WINDOW, 0)
        wv = lax.dynamic_slice_in_dim(wvalid, start, nq + WINDOW, 0)
        dist_w = qf - wp[None, :]
        s = jnp.einsum('qhd,kd->hqk', qb, wk[:, :hd]).astype(jnp.float32) * scale + t5_bias(table, dist_w)
        pw = masked_softmax(s, (dist_w >= 0) & (dist_w < WINDOW) & wv[None, :])
        o_w = jnp.einsum('hqk,kd->qhd', pw.astype(wk.dtype), wk[:, hd:])
        o = gb[..., 0:1] * o_c + gb[..., 1:2] * o_s + gb[..., 2:3] * o_w
        return o.reshape(nq, H_NSA * hd)

    return sweep(blk, P, q, gate)


def dsa_attend(q, q_idx, w_idx, rows, P, table):
    hd = HEAD_DIM
    lk = rows.shape[0]
    k, v, k_idx = rows[:, :hd], rows[:, hd:2 * hd], rows[:, 2 * hd:]
    n_keep = min(TOPK_KEYS, lk // 4)
    kpos = jnp.arange(lk, dtype=jnp.int32)
    scale = hd ** -0.5

    def blk(start, qpos, qb, qib, wb):
        qf = qpos[:, None]
        rel = jax.nn.relu(jnp.einsum('qhd,kd->qhk', qib, k_idx).astype(jnp.float32))
        isc = jnp.einsum('qh,qhk->qk', wb.astype(jnp.float32), rel)
        isc = jnp.where(kpos[None, :] <= qf, isc, -jnp.inf)
        top_s, top_i = lax.top_k(isc, n_keep)
        s = jnp.einsum('qhd,qkd->hqk', qb, k[top_i]).astype(jnp.float32) * scale + t5_bias(table, qf - top_i)
        pr = masked_softmax(s, (top_s > -jnp.inf)[None])
        return jnp.einsum('hqk,qkd->qhd', pr.astype(v.dtype), v[top_i]).reshape(qb.shape[0], H_DSA * hd)

    return sweep(blk, P, q, q_idx, w_idx)


def attend_seq(qs, rows, past, P, w_len, p, table):
    mla_q, sb_q, nsa_q, nsa_g, dsa_q, q_idx, w_idx = qs
    mla_r, sb_r, nsa_r, win_r, dsa_r = [jnp.concatenate([a, b], axis=0) for a, b in zip(past, rows)]
    win_r = jnp.pad(win_r, ((WINDOW - w_len, 0), (0, 0)))
    o_mla = mla_attend(mla_q, mla_r, P, p['w_mla_ukv'], p['g_mla_k'][:DN_MLA])
    o_sb = sb_attend(sb_q, sb_r, P)
    o_nsa = nsa_attend(nsa_q, nsa_g, nsa_r, win_r, P, w_len, p, table[:, :H_NSA])
    o_dsa = dsa_attend(dsa_q, q_idx, w_idx, dsa_r, P, table[:, H_NSA:])
    return jnp.stack([o_mla, o_sb, o_nsa, o_dsa], axis=1)


def merge(o, gates, w_br, w_out):
    br = jnp.einsum('blmc,mcd->blmd', o, w_br)
    return jnp.sum(gates * br, axis=2) @ w_out


def moe_ffn(h, w_router, b_router, w_gu, b_gu, w_down, b_down):
    shp = h.shape
    xf = h.reshape(-1, shp[-1])
    t = xf.shape[0]
    logits = (xf @ w_router + b_router).astype(jnp.float32)
    top_v, top_e = lax.top_k(logits, TOP_K)
    wts = jax.nn.softmax(top_v, axis=-1).astype(h.dtype)
    a = t * TOP_K
    flat_e = top_e.reshape(-1)
    order = jnp.argsort(flat_e)
    sorted_e = flat_e[order]
    counts = jnp.zeros((N_EXPERTS,), jnp.int32).at[flat_e].add(1)
    first = jnp.cumsum(counts) - counts
    padded = (counts + MOE_BLOCK - 1) // MOE_BLOCK * MOE_BLOCK
    pad_end = jnp.cumsum(padded)
    pad_start = pad_end - padded
    dest = (pad_start[sorted_e] + jnp.arange(a, dtype=jnp.int32) - first[sorted_e]).astype(jnp.int32)
    nb = -(-a // MOE_BLOCK) + N_EXPERTS
    slot_tok = jnp.full((nb * MOE_BLOCK,), t, jnp.int32).at[dest].set((order // TOP_K).astype(jnp.int32))
    blk_e = jnp.minimum(jnp.searchsorted(pad_end, jnp.arange(nb, dtype=jnp.int32) * MOE_BLOCK, side='right'),
                        N_EXPERTS - 1).astype(jnp.int32)
    xpad = jnp.concatenate([xf, jnp.zeros((1, xf.shape[1]), xf.dtype)], axis=0)

    def body(args):
        b, e = args
        xb = xpad[lax.dynamic_slice_in_dim(slot_tok, b * MOE_BLOCK, MOE_BLOCK, 0)]
        gu = xb @ w_gu[e] + b_gu[e]
        glu = jnp.minimum(gu[:, 0::2], SWIGLU_LIMIT)
        lin = jnp.clip(gu[:, 1::2], -SWIGLU_LIMIT, SWIGLU_LIMIT)
        act = glu * jax.nn.sigmoid(SWIGLU_ALPHA * glu) * (lin + 1.0)
        return act @ w_down[e] + b_down[e]

    out = lax.map(body, (jnp.arange(nb, dtype=jnp.int32), blk_e)).reshape(nb * MOE_BLOCK, -1)
    slot_of = jnp.zeros((a,), jnp.int32).at[order].set(dest)
    y = jnp.einsum('tk,tkd->td', wts, out[slot_of].reshape(t, TOP_K, -1))
    return y.reshape(shp)


def kernel(x_prompt, x_sample, cache_mla, cache_sb, cache_nsa, cache_dsa, state_nsa_win, page_table,
           c_prompt, c_sample, w_ada, b_ada, g_norm_mix, g_norm_ffn, w_in, g_mla_qa, w_mla_uq, g_mla_kva,
           w_mla_ukv, g_mla_q, g_mla_k, w_cmp1, w_cmp2, pe_cmp, g_nsa_q, g_nsa_k, g_dsa_q, g_dsa_k, rel_bias,
           w_br, w_out, w_router, b_router, w_gu, b_gu, w_down, b_down):
    seq_p = x_prompt.shape[1]
    seq_s = x_sample.shape[1]
    past_len = page_table.shape[1] * cache_mla.shape[2]
    w_len = state_nsa_win.shape[2]
    w_len_p = min(WINDOW, seq_p)
    pos_p = jnp.arange(seq_p, dtype=jnp.int32)
    pos_s = past_len + jnp.arange(seq_s, dtype=jnp.int32)
    xp, xs = x_prompt, x_sample
    out_p = []
    out_s = []
    for l in range(DEPTH):
        p = {'w_in': w_in[l], 'g_mla_qa': g_mla_qa[l], 'w_mla_uq': w_mla_uq[l], 'g_mla_kva': g_mla_kva[l],
             'w_mla_ukv': w_mla_ukv[l], 'g_mla_q': g_mla_q[l], 'g_mla_k': g_mla_k[l], 'w_cmp1': w_cmp1[l],
             'w_cmp2': w_cmp2[l], 'pe_cmp': pe_cmp[l], 'g_nsa_q': g_nsa_q[l], 'g_nsa_k': g_nsa_k[l],
             'g_dsa_q': g_dsa_q[l], 'g_dsa_k': g_dsa_k[l]}

        def prompt_one(qs, rows):
            past = tuple(jnp.zeros((0, r.shape[-1]), r.dtype) for r in rows)
            return attend_seq(qs, rows, past, 0, 0, p, rel_bias)

        def sample_one(args):
            qs, rows, prow, wbuf = args
            past = (cache_mla[l, prow].reshape(-1, MLA_KV_W), cache_sb[l, prow].reshape(-1, SB_KV_W),
                    cache_nsa[l, prow].reshape(-1, NSA_KV_W), wbuf, cache_dsa[l, prow].reshape(-1, DSA_KV_W))
            return attend_seq(qs, rows, past, past_len, w_len, p, rel_bias)

        mp = modulation(c_prompt, w_ada[l], b_ada[l])
        ms = modulation(c_sample, w_ada[l], b_ada[l])
        qs_p, rows_p, gates_p = project(adaln(xp, g_norm_mix[l], mp[0], mp[1]), pos_p, p)
        o_p = jax.vmap(prompt_one)(qs_p, rows_p)
        xp = xp + mp[2] * merge(o_p, gates_p, w_br[l], w_out[l])
        qs_s, rows_s, gates_s = project(adaln(xs, g_norm_mix[l], ms[0], ms[1]), pos_s, p)
        o_s = lax.map(sample_one, (qs_s, rows_s, page_table, state_nsa_win[l]))
        xs = xs + ms[2] * merge(o_s, gates_s, w_br[l], w_out[l])
        xp = xp + mp[5] * moe_ffn(adaln(xp, g_norm_ffn[l], mp[3], mp[4]), w_router[l], b_router[l],
                                  w_gu[l], b_gu[l], w_down[l], b_down[l])
        xs = xs + ms[5] * moe_ffn(adaln(xs, g_norm_ffn[l], ms[3], ms[4]), w_router[l], b_router[l],
                                  w_gu[l], b_gu[l], w_down[l], b_down[l])
        out_p.append((rows_p[0], rows_p[1], rows_p[2], rows_p[4], rows_p[3][:, seq_p - w_len_p:]))
        win_s = jnp.concatenate([state_nsa_win[l], rows_s[3]], axis=1)
        out_s.append((rows_s[0], rows_s[1], rows_s[2], rows_s[4], win_s[:, win_s.shape[1] - w_len:]))
    new_p = [jnp.stack([o[i] for o in out_p], axis=0) for i in range(5)]
    new_s = [jnp.stack([o[i] for o in out_s], axis=0) for i in range(5)]
    return (xp, xs, *new_p, *new_s)
```

```python
import functools
import math
import jax, jax.numpy as jnp
from jax import lax
import numpy as np
from jax.experimental import pallas as pl
from jax.experimental.pallas import tpu as pltpu

D_MODEL = 1024
DEPTH = 2
HEAD_DIM = 64
H_MLA = 4
Q_LORA = 256
KV_LORA = 128
DN_MLA = 64
DR_MLA = 32
DV_MLA = 64
ROPE_THETA = 10000.0
H_SB = 4
H_NSA = 4
CMP_STRIDE = 16
CMP_BLOCK = 2 * CMP_STRIDE
CMP_HIDDEN = 128
SLC_BLOCK = 64
N_SELECT = 16
N_LOCAL = 2
WINDOW = 512
H_DSA = 4
H_IDX = 4
D_IDX = 32
TOPK_KEYS = 256
NUM_BUCKETS = 32
T5_MAX_DIST = 128
N_EXPERTS = 32
TOP_K = 4
D_FF = 512
SWIGLU_LIMIT = 7.0
SWIGLU_ALPHA = 1.702
MOE_BLOCK = 128
QBLOCK = 128
N_BRANCH = 4
BR_WIDTH = 256
EPS = 1e-6
MLA_KV_W = KV_LORA + DR_MLA
SB_KV_W = 2 * HEAD_DIM
NSA_KV_W = 4 * HEAD_DIM
WIN_KV_W = 2 * HEAD_DIM
DSA_KV_W = 2 * HEAD_DIM + D_IDX
IN_SIZES = (Q_LORA, MLA_KV_W, H_SB * HEAD_DIM, SB_KV_W, H_NSA * HEAD_DIM, NSA_KV_W + WIN_KV_W, 3 * H_NSA,
            H_DSA * HEAD_DIM, 2 * HEAD_DIM, H_IDX * D_IDX, D_IDX, H_IDX, N_BRANCH * D_MODEL)

MXU_DTYPE = jnp.bfloat16
F32 = jnp.float32
NEG_INF = float("-inf")
LANES = 128
TILE = LANES
VMEM_LIMIT_BYTES = 56 * 1024 * 1024
MOE_ROWS = 256
INT_MIN = -2 ** 31

_SEG_PAD = (256, 256, 256, 128, 256, 384, 128, 256, 128, 128, 0, 0)
C_MQA, C_MKV, C_SQ, C_SKV, C_NQ, C_NKV, C_NG, C_DQ, C_DKV, C_DIQ, C_DIK = (
    0, 256, 512, 768, 896, 1152, 1536, 1664, 1920, 2048, 2176)
C_DIW = C_DIK + D_IDX
N_SMALL = 2304


def _mx(a):
    return a.astype(MXU_DTYPE)


def _dot(a, b):
    return jnp.dot(_mx(a), _mx(b), preferred_element_type=F32)


def _dot_nt(a, b):
    return lax.dot_general(_mx(a), _mx(b), (((1,), (1,)), ((), ())), preferred_element_type=F32)


def _exact_dot(a, b):
    if MXU_DTYPE == F32:
        return jnp.dot(a, b.astype(F32), preferred_element_type=F32)
    hi = a.astype(MXU_DTYPE)
    r1 = a - hi.astype(F32)
    mid = r1.astype(MXU_DTYPE)
    lo = (r1 - mid.astype(F32)).astype(MXU_DTYPE)
    f = lambda u: jnp.dot(u, b, preferred_element_type=F32)
    return f(hi) + f(mid) + f(lo)


def _cparams(*sem):
    return pltpu.CompilerParams(dimension_semantics=sem, vmem_limit_bytes=VMEM_LIMIT_BYTES)


def _adaln_rows(x, g, sh, sc):
    y = x * lax.rsqrt(jnp.mean(x * x, axis=-1, keepdims=True) + EPS)
    return y * g * (1.0 + sc) + sh


def _proj_in_kernel(x_ref, g_ref, sh_ref, sc_ref, w_ref, ys_ref, gate_ref):
    h = _mx(_adaln_rows(x_ref[...], g_ref[...], sh_ref[...], sc_ref[...]))
    ys_ref[...] = jnp.dot(h, w_ref[:, :N_SMALL], preferred_element_type=F32)
    for c in range(N_BRANCH):
        lo = N_SMALL + c * D_MODEL
        gate_ref[:, c * D_MODEL:(c + 1) * D_MODEL] = jax.nn.sigmoid(
            jnp.dot(h, w_ref[:, lo:lo + D_MODEL], preferred_element_type=F32))


def _mod_specs(mod, tm, tiles_per_group):
    r = mod.shape[1]
    return pl.BlockSpec((None, r, mod.shape[2]), lambda i: (i // tiles_per_group, 0, 0))


def proj_in(x2d, g, sh, sc, w_p, tm, tiles_per_group):
    t, d = x2d.shape
    n = w_p.shape[1]
    return pl.pallas_call(
        _proj_in_kernel,
        grid=(t // tm,),
        in_specs=[pl.BlockSpec((tm, d), lambda i: (i, 0)),
                  pl.BlockSpec((1, d), lambda i: (0, 0)),
                  _mod_specs(sh, tm, tiles_per_group), _mod_specs(sc, tm, tiles_per_group),
                  pl.BlockSpec((d, n), lambda i: (0, 0))],
        out_specs=[pl.BlockSpec((tm, N_SMALL), lambda i: (i, 0)),
                   pl.BlockSpec((tm, N_BRANCH * D_MODEL), lambda i: (i, 0))],
        out_shape=[jax.ShapeDtypeStruct((t, N_SMALL), F32), jax.ShapeDtypeStruct((t, N_BRANCH * D_MODEL), F32)],
        compiler_params=_cparams("parallel"),
        name="proj_in",
    )(x2d, g.reshape(1, d), sh, sc, w_p)


def _lin_kernel(*refs, pro, has_g, has_b):
    refs = list(refs)
    x_ref = refs.pop(0)
    g_ref = refs.pop(0) if has_g else None
    w_ref = refs.pop(0)
    b_ref = refs.pop(0) if has_b else None
    o_ref = refs.pop(0)
    x = x_ref[...]
    if pro == "silu":
        x = x * jax.nn.sigmoid(x)
    elif pro == "rms":
        x = x * lax.rsqrt(jnp.mean(x * x, axis=-1, keepdims=True) + EPS) * g_ref[...]
    y = _dot(x, w_ref[...])
    if has_b:
        y = y + b_ref[...]
    o_ref[...] = y


def lin(x, w, g=None, b=None, pro="none", tm=512, tn=None):
    m, k = x.shape
    n = w.shape[1]
    tm = min(tm, m)
    tn = n if tn is None else tn
    args = [x]
    specs = [pl.BlockSpec((tm, k), lambda i, j: (i, 0))]
    if g is not None:
        args.append(g.reshape(1, k))
        specs.append(pl.BlockSpec((1, k), lambda i, j: (0, 0)))
    args.append(_mx(w))
    specs.append(pl.BlockSpec((k, tn), lambda i, j: (0, j)))
    if b is not None:
        args.append(b.reshape(1, n))
        specs.append(pl.BlockSpec((1, tn), lambda i, j: (0, j)))
    return pl.pallas_call(
        functools.partial(_lin_kernel, pro=pro, has_g=g is not None, has_b=b is not None),
        grid=(m // tm, n // tn),
        in_specs=specs,
        out_specs=pl.BlockSpec((tm, tn), lambda i, j: (i, j)),
        out_shape=jax.ShapeDtypeStruct((m, n), F32),
        compiler_params=_cparams("parallel", "parallel"),
        name="lin_" + pro,
    )(*args)


def _router_kernel(x_ref, g_ref, sh_ref, sc_ref, w_ref, b_ref, h_ref, lg_ref):
    h = _adaln_rows(x_ref[...], g_ref[...], sh_ref[...], sc_ref[...])
    h_ref[...] = h.astype(h_ref.dtype)
    lg_ref[...] = _dot(h, w_ref[...]) + b_ref[...]


def router(x2d, g, sh, sc, w_r, b_r, tm, tiles_per_group):
    t, d = x2d.shape
    wp = jnp.pad(_mx(w_r), ((0, 0), (0, LANES - N_EXPERTS)))
    bp = jnp.pad(b_r.reshape(1, -1), ((0, 0), (0, LANES - N_EXPERTS)), constant_values=NEG_INF)
    return pl.pallas_call(
        _router_kernel,
        grid=(t // tm,),
        in_specs=[pl.BlockSpec((tm, d), lambda i: (i, 0)),
                  pl.BlockSpec((1, d), lambda i: (0, 0)),
                  _mod_specs(sh, tm, tiles_per_group), _mod_specs(sc, tm, tiles_per_group),
                  pl.BlockSpec((d, LANES), lambda i: (0, 0)),
                  pl.BlockSpec((1, LANES), lambda i: (0, 0))],
        out_specs=[pl.BlockSpec((tm, d), lambda i: (i, 0)), pl.BlockSpec((tm, LANES), lambda i: (i, 0))],
        out_shape=[jax.ShapeDtypeStruct((t, d), MXU_DTYPE), jax.ShapeDtypeStruct((t, LANES), F32)],
        compiler_params=_cparams("parallel"),
        name="router",
    )(x2d, g.reshape(1, d), sh, sc, wp, bp)


def _moe_kernel(be_ref, x_ref, wg_ref, bg_ref, wd_ref, bd_ref, o_ref):
    gu = _dot(x_ref[...], wg_ref[...]) + bg_ref[...]
    glu = jnp.minimum(gu[:, :D_FF], SWIGLU_LIMIT)
    lin_ = jnp.clip(gu[:, D_FF:], -SWIGLU_LIMIT, SWIGLU_LIMIT)
    act = glu * jax.nn.sigmoid(SWIGLU_ALPHA * glu) * (lin_ + 1.0)
    o_ref[...] = _dot(act, wd_ref[...]) + bd_ref[...]


def moe_blocks(xg, blk_e, wg, bg, wd, bd):
    n, d = xg.shape
    nb = n // MOE_ROWS
    grid_spec = pltpu.PrefetchScalarGridSpec(
        num_scalar_prefetch=1,
        grid=(nb,),
        in_specs=[pl.BlockSpec((MOE_ROWS, d), lambda b, be: (b, 0)),
                  pl.BlockSpec((None, d, 2 * D_FF), lambda b, be: (be[b], 0, 0)),
                  pl.BlockSpec((None, 1, 2 * D_FF), lambda b, be: (be[b], 0, 0)),
                  pl.BlockSpec((None, D_FF, d), lambda b, be: (be[b], 0, 0)),
                  pl.BlockSpec((None, 1, d), lambda b, be: (be[b], 0, 0))],
        out_specs=pl.BlockSpec((MOE_ROWS, d), lambda b, be: (b, 0)))
    return pl.pallas_call(
        _moe_kernel, grid_spec=grid_spec,
        out_shape=jax.ShapeDtypeStruct((n, d), F32),
        compiler_params=_cparams("arbitrary"),
        name="moe_ffn",
    )(blk_e, xg, wg, bg, wd, bd)


def moe_ffn(h, logits, wg, bg, wd, bd):
    t, d = h.shape
    top_v, top_e = lax.top_k(logits, TOP_K)
    wts = jax.nn.softmax(top_v, axis=-1)
    a = t * TOP_K
    flat_e = top_e.reshape(-1)
    onehot = (flat_e[:, None] == jnp.arange(N_EXPERTS, dtype=jnp.int32)[None, :]).astype(jnp.int32)
    rank = jnp.sum((jnp.cumsum(onehot, axis=0) - onehot) * onehot, axis=1)
    counts = jnp.sum(onehot, axis=0)
    padded = (counts + MOE_ROWS - 1) // MOE_ROWS * MOE_ROWS
    pad_end = jnp.cumsum(padded)
    pad_start = pad_end - padded
    dest = (pad_start[flat_e] + rank).astype(jnp.int32)
    nb = -(-a // MOE_ROWS) + N_EXPERTS
    slot_tok = jnp.full((nb * MOE_ROWS,), t, jnp.int32).at[dest].set(jnp.arange(a, dtype=jnp.int32) // TOP_K)
    blk_e = jnp.minimum(jnp.searchsorted(pad_end, jnp.arange(nb, dtype=jnp.int32) * MOE_ROWS, side='right'),
                        N_EXPERTS - 1).astype(jnp.int32)
    hpad = jnp.concatenate([h, jnp.zeros((1, d), h.dtype)], axis=0)
    out = moe_blocks(hpad[slot_tok], blk_e, wg, bg, wd, bd)
    return jnp.einsum('tk,tkd->td', wts, out[dest].reshape(t, TOP_K, d))


def _merge_kernel(o0_ref, o1_ref, o2_ref, o3_ref, gates_ref, wbr_ref, wout_ref, x_ref, gm_ref, out_ref):
    acc = None
    for b, o_ref in enumerate((o0_ref, o1_ref, o2_ref, o3_ref)):
        t = gates_ref[:, b * D_MODEL:(b + 1) * D_MODEL] * _dot(o_ref[...], wbr_ref[b])
        acc = t if acc is None else acc + t
    out_ref[...] = x_ref[...] + gm_ref[...] * _dot(acc, wout_ref[...])


def merge(o4, gates, w_br, w_out, x2d, gm, tm, tiles_per_group):
    t, d = x2d.shape
    ospec = pl.BlockSpec((tm, BR_WIDTH), lambda i: (i, 0))
    return pl.pallas_call(
        _merge_kernel,
        grid=(t // tm,),
        in_specs=[ospec, ospec, ospec, ospec,
                  pl.BlockSpec((tm, N_BRANCH * d), lambda i: (i, 0)),
                  pl.BlockSpec((N_BRANCH, BR_WIDTH, d), lambda i: (0, 0, 0)),
                  pl.BlockSpec((d, d), lambda i: (0, 0)),
                  pl.BlockSpec((tm, d), lambda i: (i, 0)),
                  _mod_specs(gm, tm, tiles_per_group)],
        out_specs=pl.BlockSpec((tm, d), lambda i: (i, 0)),
        out_shape=jax.ShapeDtypeStruct((t, d), F32),
        compiler_params=_cparams("parallel"),
        name="merge",
    )(*o4, gates, _mx(w_br), _mx(w_out), x2d, gm)


def _order_key(x):
    bits = lax.bitcast_convert_type(x, jnp.int32)
    return jnp.where(bits >= 0, bits, bits ^ jnp.int32(0x7FFFFFFF))


KEY_NEG_INF = int(np.array(-np.inf, np.float32).view(np.int32)) ^ 0x7FFFFFFF
KEY_NEG_INF = KEY_NEG_INF - 2 ** 32 if KEY_NEG_INF >= 2 ** 31 else KEY_NEG_INF


def _kth_threshold(count_ge, k, rows):
    zero = jnp.zeros((rows, 1), jnp.int32)
    t = jnp.where(count_ge(zero) >= k, zero, jnp.full((rows, 1), INT_MIN, jnp.int32))

    def step(s, t):
        cand = t + (jnp.int32(1) << (30 - s))
        return jnp.where(count_ge(cand) >= k, cand, t)

    return lax.fori_loop(0, 31, step, t)


def _lower_tri(n):
    return (lax.broadcasted_iota(jnp.int32, (n, n), 0) < lax.broadcasted_iota(jnp.int32, (n, n), 1)).astype(MXU_DTYPE)


def _topk_mask_row(score, k):
    r, c = score.shape
    key = _order_key(score)

    def count_ge(t):
        return jnp.sum(jnp.where(key >= t, 1.0, 0.0), axis=1, keepdims=True)

    t = _kth_threshold(count_ge, float(k), r)
    gt = key > t
    eq = key == t
    need = float(k) - jnp.sum(jnp.where(gt, 1.0, 0.0), axis=1, keepdims=True)
    prefix = jnp.dot(jnp.where(eq, 1.0, 0.0).astype(MXU_DTYPE), _lower_tri(c), preferred_element_type=F32)
    return (gt | (eq & (prefix < need))) & (key > KEY_NEG_INF)


def _softmax_step(s, ok, m, l, acc, v):
    s = jnp.where(ok, s, NEG_INF)
    m_new = jnp.maximum(m, jnp.max(s, axis=1, keepdims=True))
    m_safe = jnp.where(m_new == NEG_INF, 0.0, m_new)
    p = jnp.exp(s - m_safe)
    alpha = jnp.exp(m - m_safe)
    return m_new, alpha * l + jnp.sum(p, axis=1, keepdims=True), alpha * acc + _dot(p, v)


def _tile_bias(tb_ref, d, heads):
    b = jnp.where(d == 0, tb_ref[0], jnp.where(d == 1, tb_ref[1], tb_ref[2]))
    return b.reshape(heads * TILE, TILE)


def _store_heads(o_ref, o, heads, rows, width):
    for h in range(heads):
        o_ref[:, h * width:(h + 1) * width] = o[h * rows:(h + 1) * rows]


def _mla_kernel(q_ref, k_ref, v_ref, o_ref, m_sc, l_sc, acc_sc, *, tq, tk, scale):
    i = pl.program_id(1)
    j = pl.program_id(2)

    @pl.when(j == 0)
    def _():
        m_sc[...] = jnp.full(m_sc.shape, NEG_INF, F32)
        l_sc[...] = jnp.zeros(l_sc.shape, F32)
        acc_sc[...] = jnp.zeros(acc_sc.shape, F32)

    @pl.when(j * tk < (i + 1) * tq)
    def _():
        qpos = i * tq + lax.broadcasted_iota(jnp.int32, (tq, tk), 0)
        kpos = j * tk + lax.broadcasted_iota(jnp.int32, (tq, tk), 1)
        causal = kpos <= qpos
        for h in range(H_MLA):
            s = _dot_nt(q_ref[:, h * LANES:(h + 1) * LANES], k_ref[:, h * LANES:(h + 1) * LANES]) * scale
            m, l, acc = _softmax_step(s, causal, m_sc[h], l_sc[h], acc_sc[h], v_ref[:, h * DV_MLA:(h + 1) * DV_MLA])
            m_sc[h] = m
            l_sc[h] = l
            acc_sc[h] = acc

    @pl.when(j == pl.num_programs(2) - 1)
    def _():
        for h in range(H_MLA):
            o_ref[:, h * DV_MLA:(h + 1) * DV_MLA] = acc_sc[h] / l_sc[h]


def mla_prompt(q, k, v):
    b, l, _ = q.shape
    tq = tk = min(256, l)
    last = lambda i: ((i + 1) * tq - 1) // tk
    return pl.pallas_call(
        functools.partial(_mla_kernel, tq=tq, tk=tk, scale=(DN_MLA + DR_MLA) ** -0.5),
        grid=(b, l // tq, l // tk),
        in_specs=[pl.BlockSpec((None, tq, H_MLA * LANES), lambda bb, i, j: (bb, i, 0)),
                  pl.BlockSpec((None, tk, H_MLA * LANES), lambda bb, i, j: (bb, jnp.minimum(j, last(i)), 0)),
                  pl.BlockSpec((None, tk, H_MLA * DV_MLA), lambda bb, i, j: (bb, jnp.minimum(j, last(i)), 0))],
        out_specs=pl.BlockSpec((None, tq, H_MLA * DV_MLA), lambda bb, i, j: (bb, i, 0)),
        out_shape=jax.ShapeDtypeStruct((b, l, H_MLA * DV_MLA), F32),
        scratch_shapes=[pltpu.VMEM((H_MLA, tq, 1), F32), pltpu.VMEM((H_MLA, tq, 1), F32),
                        pltpu.VMEM((H_MLA, tq, DV_MLA), F32)],
        compiler_params=_cparams("parallel", "parallel", "arbitrary"),
        name="mla_prompt",
    )(q, k, v)


def _sb_kernel(q_ref, k_ref, v_ref, tri_ref, o_ref, carry_sc, acc_sc, *, tq, nsub):
    i = pl.program_id(1)
    s_id = pl.program_id(2)
    ks = nsub * TILE
    kb = ((i + 1) * tq - 1) // ks - s_id
    rows = H_SB * tq

    @pl.when(s_id == 0)
    def _():
        carry_sc[...] = jnp.zeros(carry_sc.shape, F32)
        acc_sc[...] = jnp.zeros(acc_sc.shape, F32)

    @pl.when(kb >= 0)
    def _():
        q = q_ref[...].reshape(rows, HEAD_DIM)
        qpos = i * tq + (lax.broadcasted_iota(jnp.int32, (rows, TILE), 0) & (tq - 1))
        lane = lax.broadcasted_iota(jnp.int32, (rows, TILE), 1)
        for u in reversed(range(nsub)):
            kstart = kb * ks + u * TILE

            @pl.when(kstart < (i + 1) * tq)
            def _():
                z = _dot_nt(q, k_ref[u]) * HEAD_DIM ** -0.5
                before = (kstart + lane) < qpos
                ls = jnp.minimum(z, 0.0) - jnp.log1p(jnp.exp(-jnp.abs(z)))
                lk = jnp.where(before, ls - z, 0.0)
                both = _exact_dot(lk, tri_ref[...])
                later = both[:, :TILE] + carry_sc[...]
                a = jnp.where(before, jnp.exp(ls + later), 0.0)
                acc_sc[...] += _dot(a, v_ref[u])
                carry_sc[...] += both[:, TILE:]

    @pl.when(s_id == pl.num_programs(2) - 1)
    def _():
        _store_heads(o_ref, acc_sc[...], H_SB, tq, HEAD_DIM)


def sb_prompt(q4, k, v):
    b, _, l, hd = q4.shape
    tq = TILE
    nsub = min(4, l // TILE)
    ks = nsub * TILE
    nt = l // ks
    idx = lax.broadcasted_iota(jnp.int32, (TILE, TILE), 0) > lax.broadcasted_iota(jnp.int32, (TILE, TILE), 1)
    tri = jnp.concatenate([idx.astype(MXU_DTYPE), jnp.ones((TILE, TILE), MXU_DTYPE)], axis=1)
    kmap = lambda bb, i, s: (bb, jnp.maximum(((i + 1) * tq - 1) // ks - s, 0), 0, 0)
    k4 = _mx(k).reshape(b, nt * nsub, TILE, hd)
    v4 = _mx(v).reshape(b, nt * nsub, TILE, hd)
    return pl.pallas_call(
        functools.partial(_sb_kernel, tq=tq, nsub=nsub),
        grid=(b, l // tq, nt),
        in_specs=[pl.BlockSpec((None, H_SB, tq, hd), lambda bb, i, s: (bb, 0, i, 0)),
                  pl.BlockSpec((None, nsub, TILE, hd), kmap),
                  pl.BlockSpec((None, nsub, TILE, hd), kmap),
                  pl.BlockSpec((TILE, 2 * TILE), lambda bb, i, s: (0, 0))],
        out_specs=pl.BlockSpec((None, tq, H_SB * hd), lambda bb, i, s: (bb, i, 0)),
        out_shape=jax.ShapeDtypeStruct((b, l, H_SB * hd), F32),
        scratch_shapes=[pltpu.VMEM((H_SB * tq, TILE), F32), pltpu.VMEM((H_SB * tq, hd), F32)],
        compiler_params=_cparams("parallel", "parallel", "arbitrary"),
        name="sb_prompt",
    )(_mx(q4), k4, v4, tri)


def _cmp_kernel(x_ref, w1_ref, w2_ref, g_ref, o_ref):
    y = _dot(jax.nn.gelu(_dot(x_ref[...], w1_ref[...])), w2_ref[...])
    yn = y * lax.rsqrt(jnp.mean(y * y, axis=-1, keepdims=True) + EPS) * g_ref[...]
    o_ref[...] = jnp.where(pl.program_id(1) == 0, yn, y)


def compress(blocks, w1, w2, g):
    b, _, nc, kdim = blocks.shape
    return pl.pallas_call(
        _cmp_kernel,
        grid=(b, 2),
        in_specs=[pl.BlockSpec((None, None, nc, kdim), lambda bb, i: (bb, i, 0, 0)),
                  pl.BlockSpec((None, kdim, CMP_HIDDEN), lambda bb, i: (i, 0, 0)),
                  pl.BlockSpec((None, CMP_HIDDEN, HEAD_DIM), lambda bb, i: (i, 0, 0)),
                  pl.BlockSpec((1, HEAD_DIM), lambda bb, i: (0, 0))],
        out_specs=pl.BlockSpec((None, None, nc, HEAD_DIM), lambda bb, i: (bb, i, 0, 0)),
        out_shape=jax.ShapeDtypeStruct((b, 2, nc, HEAD_DIM), F32),
        compiler_params=_cparams("parallel", "parallel"),
        name="nsa_compress",
    )(blocks, _mx(w1), _mx(w2), g.reshape(1, HEAD_DIM))


def _nsa_kernel(q_ref, gate_ref, kc_ref, vc_ref, bc_ref, ks_ref, vs_ref, kw_ref, vw_ref, smat_ref, tb_ref, o_ref,
                *, n_sel):
    i = pl.program_id(1)
    rows = H_NSA * TILE
    scale = HEAD_DIM ** -0.5
    ncp = kc_ref.shape[0]
    nsp = smat_ref.shape[1]
    q = q_ref[...].reshape(rows, HEAD_DIM)
    qp4 = i * TILE + (lax.broadcasted_iota(jnp.int32, (rows, 1), 0) & (TILE - 1))
    qp = i * TILE + lax.broadcasted_iota(jnp.int32, (TILE, 1), 0)
    lane = lax.broadcasted_iota(jnp.int32, (TILE, TILE), 1)
    rep4 = lambda x: jnp.concatenate([x] * H_NSA, axis=0)

    cmp_end = lax.broadcasted_iota(jnp.int32, (1, ncp), 1) * CMP_STRIDE + (CMP_BLOCK - 1)
    s = _dot_nt(q, kc_ref[...]) * scale + bc_ref[...].reshape(rows, ncp)
    s = jnp.where(qp4 >= cmp_end, s, NEG_INF)
    m = jnp.max(s, axis=1, keepdims=True)
    e = jnp.exp(s - jnp.where(m == NEG_INF, 0.0, m))
    d = jnp.sum(e, axis=1, keepdims=True)
    pc = e / jnp.where(d > 0, d, 1.0)
    o_c = _dot(pc, vc_ref[...])

    imp = ((pc[0:TILE] + pc[TILE:2 * TILE]) + pc[2 * TILE:3 * TILE]) + pc[3 * TILE:4 * TILE]
    score = _exact_dot(imp, smat_ref[...])
    blk = lax.broadcasted_iota(jnp.int32, (TILE, nsp), 1)
    back = qp // SLC_BLOCK - blk
    forced = (blk == 0) | ((back >= 0) & (back < N_LOCAL))
    score = jnp.where(back >= 0, jnp.where(forced, jnp.inf, score), NEG_INF)
    sel_f = jnp.where(_topk_mask_row(score, n_sel), 1.0, 0.0)

    zero_state = (jnp.full((rows, 1), NEG_INF, F32), jnp.zeros((rows, 1), F32), jnp.zeros((rows, HEAD_DIM), F32))
    blk_row = lax.broadcasted_iota(jnp.int32, (nsp, TILE), 0)
    half = lax.broadcasted_iota(jnp.int32, (nsp, TILE), 1) // SLC_BLOCK

    def slc_body(j, st):
        expand = jnp.where(blk_row == j * (TILE // SLC_BLOCK) + half, 1.0, 0.0)
        picked = _dot(sel_f, expand) > 0.5
        ok = picked & ((j * TILE + lane) <= qp)
        sc = _dot_nt(q, ks_ref[j]) * scale + _tile_bias(tb_ref, i - j, H_NSA)
        return _softmax_step(sc, rep4(ok), *st, vs_ref[j])

    _, l_s, acc_s = lax.fori_loop(0, i + 1, slc_body, zero_state)
    o_s = acc_s / jnp.where(l_s > 0, l_s, 1.0)

    def win_body(j, st):
        dist = qp - (j * TILE + lane)
        ok = (dist >= 0) & (dist < WINDOW)
        sc = _dot_nt(q, kw_ref[j]) * scale + _tile_bias(tb_ref, i - j, H_NSA)
        return _softmax_step(sc, rep4(ok), *st, vw_ref[j])

    _, l_w, acc_w = lax.fori_loop(jnp.maximum(i - WINDOW // TILE, 0), i + 1, win_body, zero_state)
    o_w = acc_w / jnp.where(l_w > 0, l_w, 1.0)

    g = gate_ref[...].reshape(rows, 3)
    o = g[:, 0:1] * o_c + g[:, 1:2] * o_s + g[:, 2:3] * o_w
    _store_heads(o_ref, o, H_NSA, TILE, HEAD_DIM)


def nsa_prompt(q4, gate4, cmp_kv, bias_c, k_slc, v_slc, k_win, v_win, smat, tb):
    b, _, l, hd = q4.shape
    nt = l // TILE
    ncp = cmp_kv.shape[2]
    nsp = smat.shape[1]
    n_sel = min(N_SELECT, l // SLC_BLOCK)
    tiles = lambda x: _mx(x).reshape(b, nt, TILE, hd)
    kv_spec = pl.BlockSpec((None, nt, TILE, hd), lambda bb, i: (bb, 0, 0, 0))
    return pl.pallas_call(
        functools.partial(_nsa_kernel, n_sel=n_sel),
        grid=(b, nt),
        in_specs=[pl.BlockSpec((None, H_NSA, TILE, hd), lambda bb, i: (bb, 0, i, 0)),
                  pl.BlockSpec((None, H_NSA, TILE, 3), lambda bb, i: (bb, 0, i, 0)),
                  pl.BlockSpec((None, None, ncp, hd), lambda bb, i: (bb, 0, 0, 0)),
                  pl.BlockSpec((None, None, ncp, hd), lambda bb, i: (bb, 1, 0, 0)),
                  pl.BlockSpec((H_NSA, TILE, ncp), lambda bb, i: (0, i, 0)),
                  kv_spec, kv_spec, kv_spec, kv_spec,
                  pl.BlockSpec((ncp, nsp), lambda bb, i: (0, 0)),
                  pl.BlockSpec((3, H_NSA, TILE, TILE), lambda bb, i: (0, 0, 0, 0))],
        out_specs=pl.BlockSpec((None, TILE, H_NSA * hd), lambda bb, i: (bb, i, 0)),
        out_shape=jax.ShapeDtypeStruct((b, l, H_NSA * hd), F32),
        compiler_params=_cparams("parallel", "parallel"),
        name="nsa_prompt",
    )(_mx(q4), gate4, _mx(cmp_kv), _mx(cmp_kv), bias_c, tiles(k_slc), tiles(v_slc), tiles(k_win), tiles(v_win),
      smat, tb)


def _dsa_kernel(q_ref, qi_ref, w_ref, kd_ref, vd_ref, ki_ref, tb_ref, o_ref, key_sc, *, n_keep):
    i = pl.program_id(1)
    rows = H_DSA * TILE
    scale = HEAD_DIM ** -0.5
    q = q_ref[...].reshape(rows, HEAD_DIM)
    qp = i * TILE + lax.broadcasted_iota(jnp.int32, (TILE, 1), 0)
    lane = lax.broadcasted_iota(jnp.int32, (TILE, TILE), 1)
    w = w_ref[...]

    def index_body(j, c):
        ki = ki_ref[j]
        isc = None
        for h in range(H_IDX):
            t = w[:, h:h + 1] * jnp.maximum(_dot_nt(qi_ref[h], ki), 0.0)
            isc = t if isc is None else isc + t
        isc = jnp.where((j * TILE + lane) <= qp, isc, NEG_INF)
        key_sc[j] = _order_key(isc)
        return c

    lax.fori_loop(0, i + 1, index_body, 0)

    def count_ge(t):
        def body(j, c):
            return c + jnp.where(key_sc[j] >= t, 1.0, 0.0)
        return jnp.sum(lax.fori_loop(0, i + 1, body, jnp.zeros((TILE, TILE), F32)), axis=1, keepdims=True)

    t = _kth_threshold(count_ge, float(n_keep), TILE)
    need = float(n_keep) - count_ge(t + 1)
    tri = _lower_tri(TILE)
    rep4 = lambda x: jnp.concatenate([x] * H_DSA, axis=0)

    def att_body(j, st):
        m, l, acc, seen = st
        key = key_sc[j]
        eq = key == t
        eq_f = jnp.where(eq, 1.0, 0.0)
        prefix = _dot(eq_f, tri) + seen
        ok = ((key > t) | (eq & (prefix < need))) & (key > KEY_NEG_INF)
        sc = _dot_nt(q, kd_ref[j]) * scale + _tile_bias(tb_ref, i - j, H_DSA)
        m, l, acc = _softmax_step(sc, rep4(ok), m, l, acc, vd_ref[j])
        return m, l, acc, seen + jnp.sum(eq_f, axis=1, keepdims=True)

    st0 = (jnp.full((rows, 1), NEG_INF, F32), jnp.zeros((rows, 1), F32), jnp.zeros((rows, HEAD_DIM), F32),
           jnp.zeros((TILE, 1), F32))
    _, l, acc, _ = lax.fori_loop(0, i + 1, att_body, st0)
    _store_heads(o_ref, acc / jnp.where(l > 0, l, 1.0), H_DSA, TILE, HEAD_DIM)


def dsa_prompt(q4, qi4, w_idx, k, v, k_idx, tb):
    b, _, l, hd = q4.shape
    nt = l // TILE
    n_keep = min(TOPK_KEYS, l // 4)
    tiles = lambda x: _mx(x).reshape(b, nt, TILE, x.shape[-1])
    kv_spec = lambda wd: pl.BlockSpec((None, nt, TILE, wd), lambda bb, i: (bb, 0, 0, 0))
    return pl.pallas_call(
        functools.partial(_dsa_kernel, n_keep=n_keep),
        grid=(b, nt),
        in_specs=[pl.BlockSpec((None, H_DSA, TILE, hd), lambda bb, i: (bb, 0, i, 0)),
                  pl.BlockSpec((None, H_IDX, TILE, D_IDX), lambda bb, i: (bb, 0, i, 0)),
                  pl.BlockSpec((None, TILE, H_IDX), lambda bb, i: (bb, i, 0)),
                  kv_spec(hd), kv_spec(hd), kv_spec(D_IDX),
                  pl.BlockSpec((3, H_DSA, TILE, TILE), lambda bb, i: (0, 0, 0, 0))],
        out_specs=pl.BlockSpec((None, TILE, H_DSA * hd), lambda bb, i: (bb, i, 0)),
        out_shape=jax.ShapeDtypeStruct((b, l, H_DSA * hd), F32),
        scratch_shapes=[pltpu.VMEM((nt, TILE, TILE), jnp.int32)],
        compiler_params=_cparams("parallel", "arbitrary"),
        name="dsa_prompt",
    )(_mx(q4), _mx(qi4), w_idx, tiles(k), tiles(v), tiles(k_idx), tb)


def rmsnorm(x, g):
    xf = x.astype(jnp.float32)
    y = xf * lax.rsqrt(jnp.mean(xf * xf, axis=-1, keepdims=True) + EPS)
    return y.astype(x.dtype) * g


def rope(x, pos):
    half = x.shape[-1] // 2
    freqs = ROPE_THETA ** (-jnp.arange(half, dtype=jnp.float32) / half)
    ang = pos.astype(jnp.float32)[..., None] * freqs
    cos, sin = jnp.cos(ang), jnp.sin(ang)
    xf = x.astype(jnp.float32)
    x1, x2 = xf[..., :half], xf[..., half:]
    return jnp.concatenate([x1 * cos - x2 * sin, x2 * cos + x1 * sin], axis=-1).astype(x.dtype)


def t5_bias(table, dist):
    n = jnp.maximum(dist, 0)
    exact = NUM_BUCKETS // 2
    big = exact + (jnp.log(jnp.maximum(n, exact).astype(jnp.float32) / exact)
                   / math.log(T5_MAX_DIST / exact) * (NUM_BUCKETS - exact)).astype(jnp.int32)
    bucket = jnp.where(n < exact, n, jnp.minimum(big, NUM_BUCKETS - 1))
    return jnp.moveaxis(table[bucket].astype(jnp.float32), -1, 0)


def masked_softmax(s, mask):
    s = jnp.where(mask, s, -jnp.inf)
    m = jnp.max(s, axis=-1, keepdims=True)
    e = jnp.exp(s - jnp.where(jnp.isfinite(m), m, 0.0))
    d = jnp.sum(e, axis=-1, keepdims=True)
    return e / jnp.where(d > 0, d, 1.0)


def sweep(fn, P, *qarrs):
    lq = qarrs[0].shape[0]
    qb = QBLOCK if lq % QBLOCK == 0 else lq
    qpos = P + jnp.arange(lq, dtype=jnp.int32)

    def body(b):
        start = b * qb
        blocks = [lax.dynamic_slice_in_dim(a, start, qb, 0) for a in qarrs]
        return fn(start, lax.dynamic_slice_in_dim(qpos, start, qb, 0), *blocks)

    out = lax.map(body, jnp.arange(lq // qb, dtype=jnp.int32))
    return out.reshape((lq,) + out.shape[2:])


def group_transforms(y, pos, p):
    B, L, _ = y.shape
    hd = HEAD_DIM
    seg = lambda c, w: y[..., c:c + w]
    mqa, mkv = seg(C_MQA, Q_LORA), seg(C_MKV, MLA_KV_W)
    nkv, dkv = seg(C_NKV, NSA_KV_W + WIN_KV_W), seg(C_DKV, 2 * hd)
    q = lin(mqa.reshape(B * L, Q_LORA), p['w_mla_uq'], g=p['g_mla_qa'], pro="rms",
            tm=min(512, B * L)).reshape(B, L, H_MLA, DN_MLA + DR_MLA)
    gq = p['g_mla_q']
    mla_q = jnp.concatenate([rmsnorm(q[..., :DN_MLA], gq[:DN_MLA]),
                             rope(rmsnorm(q[..., DN_MLA:], gq[DN_MLA:]), pos[:, None])], axis=-1)
    mla_rows = jnp.concatenate([rmsnorm(mkv[..., :KV_LORA], p['g_mla_kva']),
                                rope(rmsnorm(mkv[..., KV_LORA:], p['g_mla_k'][DN_MLA:]), pos)], axis=-1)
    sb_q = seg(C_SQ, H_SB * hd).reshape(B, L, H_SB, hd)
    nsa_q = rmsnorm(seg(C_NQ, H_NSA * hd).reshape(B, L, H_NSA, hd), p['g_nsa_q'])
    gk = p['g_nsa_k']
    nsa_rows = jnp.concatenate([nkv[..., :2 * hd], rmsnorm(nkv[..., 2 * hd:3 * hd], gk[1]), nkv[..., 3 * hd:4 * hd]], axis=-1)
    win_rows = jnp.concatenate([rmsnorm(nkv[..., 4 * hd:5 * hd], gk[2]), nkv[..., 5 * hd:]], axis=-1)
    nsa_g = jax.nn.sigmoid(seg(C_NG, 3 * H_NSA).reshape(B, L, H_NSA, 3))
    dsa_q = rmsnorm(seg(C_DQ, H_DSA * hd).reshape(B, L, H_DSA, hd), p['g_dsa_q'])
    dsa_rows = jnp.concatenate([rmsnorm(dkv[..., :hd], p['g_dsa_k']), dkv[..., hd:], seg(C_DIK, D_IDX)], axis=-1)
    q_idx = seg(C_DIQ, H_IDX * D_IDX).reshape(B, L, H_IDX, D_IDX)
    qs = (mla_q, sb_q, nsa_q, nsa_g, dsa_q, q_idx, seg(C_DIW, H_IDX))
    rows = (mla_rows, seg(C_SKV, SB_KV_W), nsa_rows, win_rows, dsa_rows)
    return qs, rows


def mla_attend(q, rows, P, w_ukv, g_kn):
    lk = rows.shape[0]
    kv = (rows[:, :KV_LORA] @ w_ukv).reshape(lk, H_MLA, DN_MLA + DV_MLA)
    k_n = rmsnorm(kv[..., :DN_MLA], g_kn)
    v = kv[..., DN_MLA:]
    k_r = rows[:, KV_LORA:]
    kpos = jnp.arange(lk, dtype=jnp.int32)
    scale = (DN_MLA + DR_MLA) ** -0.5

    def blk(start, qpos, qb):
        s = (jnp.einsum('qhd,khd->hqk', qb[..., :DN_MLA], k_n).astype(jnp.float32)
             + jnp.einsum('qhd,kd->hqk', qb[..., DN_MLA:], k_r).astype(jnp.float32)) * scale
        pr = masked_softmax(s, kpos[None, :] <= qpos[:, None])
        return jnp.einsum('hqk,khd->qhd', pr.astype(v.dtype), v).reshape(qb.shape[0], H_MLA * DV_MLA)

    return sweep(blk, P, q)


def sb_attend(q, rows, P):
    lk = rows.shape[0]
    k, v = rows[:, :HEAD_DIM], rows[:, HEAD_DIM:]
    kpos = jnp.arange(lk, dtype=jnp.int32)

    def blk(start, qpos, qb):
        z = jnp.einsum('qhd,kd->hqk', qb, k).astype(jnp.float32) * HEAD_DIM ** -0.5
        before = kpos[None, :] < qpos[:, None]
        log_beta = jnp.where(before, jax.nn.log_sigmoid(z), -jnp.inf)
        log_keep = jnp.where(before, jax.nn.log_sigmoid(-z), 0.0)
        later = lax.cumsum(log_keep, axis=2, reverse=True) - log_keep
        a = jnp.exp(log_beta + later)
        return jnp.einsum('hqk,kd->qhd', a.astype(v.dtype), v).reshape(qb.shape[0], H_SB * HEAD_DIM)

    return sweep(blk, P, q)


def nsa_attend(q, gate, nsa_rows, win_rows, P, w_len, p, table):
    hd = HEAD_DIM
    lk = nsa_rows.shape[0]
    scale = hd ** -0.5
    n_chunk = lk // CMP_STRIDE
    n_cmp = n_chunk - 1

    def compress_(x, i):
        ch = x[: n_chunk * CMP_STRIDE].reshape(n_chunk, CMP_STRIDE, hd)
        blocks = jnp.concatenate([ch[:-1], ch[1:]], axis=1) + p['pe_cmp'][i]
        return jax.nn.gelu(blocks.reshape(n_cmp, CMP_BLOCK * hd) @ p['w_cmp1'][i]) @ p['w_cmp2'][i]

    k_cmp = rmsnorm(compress_(nsa_rows[:, :hd], 0), p['g_nsa_k'][0])
    v_cmp = compress_(nsa_rows[:, hd:2 * hd], 1)
    cmp_end = jnp.arange(n_cmp, dtype=jnp.int32) * CMP_STRIDE + (CMP_BLOCK - 1)
    n_slc = -(-lk // SLC_BLOCK)
    n_sel = min(N_SELECT, n_slc)
    slc = jnp.pad(nsa_rows[:, 2 * hd:], ((0, n_slc * SLC_BLOCK - lk), (0, 0))).reshape(n_slc, SLC_BLOCK, 2 * hd)
    ratio = SLC_BLOCK // CMP_STRIDE
    span = CMP_BLOCK // CMP_STRIDE
    blk_ids = jnp.arange(n_slc, dtype=jnp.int32)
    lw = win_rows.shape[0]
    wpos = P - WINDOW + jnp.arange(lw, dtype=jnp.int32)
    wvalid = wpos >= P - w_len

    def blk(start, qpos, qb, gb):
        nq = qb.shape[0]
        qf = qpos[:, None]
        dist_c = qf - cmp_end[None, :]
        s = jnp.einsum('qhd,kd->hqk', qb, k_cmp).astype(jnp.float32) * scale + t5_bias(table, dist_c)
        pc = masked_softmax(s, dist_c >= 0)
        o_c = jnp.einsum('hqk,kd->qhd', pc.astype(v_cmp.dtype), v_cmp)
        imp = jnp.pad(pc.sum(0), ((0, 0), (span - 1, ratio * (n_slc + 1) - (span - 1) - n_cmp)))
        score = jnp.zeros((nq, n_slc), jnp.float32)
        for m in range(ratio):
            for n in range(span):
                score = score + imp[:, m + n: m + n + ratio * n_slc: ratio]
        back = qf // SLC_BLOCK - blk_ids[None, :]
        forced = (blk_ids[None, :] == 0) | ((back >= 0) & (back < N_LOCAL))
        score = jnp.where(back >= 0, jnp.where(forced, jnp.inf, score), -jnp.inf)
        top_s, top_i = lax.top_k(score, n_sel)
        g = slc[top_i].reshape(nq, n_sel * SLC_BLOCK, 2 * hd)
        kpos = (top_i[..., None] * SLC_BLOCK + jnp.arange(SLC_BLOCK, dtype=jnp.int32)).reshape(nq, -1)
        ok = jnp.repeat(top_s > -jnp.inf, SLC_BLOCK, axis=1) & (kpos <= qf)
        s = jnp.einsum('qhd,qkd->hqk', qb, g[..., :hd]).astype(jnp.float32) * scale + t5_bias(table, qf - kpos)
        ps = masked_softmax(s, ok[None])
        o_s = jnp.einsum('hqk,qkd->qhd', ps.astype(g.dtype), g[..., hd:])
        wk = lax.dynamic_slice_in_dim(win_rows, start, nq + WINDOW, 0)
        wp = lax.dynamic_slice_in_dim(wpos, start, nq + WINDOW, 0)
        wv = lax.dynamic_slice_in_dim(wvalid, start, nq + WINDOW, 0)
        dist_w = qf - wp[None, :]
        s = jnp.einsum('qhd,kd->hqk', qb, wk[:, :hd]).astype(jnp.float32) * scale + t5_bias(table, dist_w)
        pw = masked_softmax(s, (dist_w >= 0) & (dist_w < WINDOW) & wv[None, :])
        o_w = jnp.einsum('hqk,kd->qhd', pw.astype(wk.dtype), wk[:, hd:])
        o = gb[..., 0:1] * o_c + gb[..., 1:2] * o_s + gb[..., 2:3] * o_w
        return o.reshape(nq, H_NSA * hd)

    return sweep(blk, P, q, gate)


def dsa_attend(q, q_idx, w_idx, rows, P, table):
    hd = HEAD_DIM
    lk = rows.shape[0]
    k, v, k_idx = rows[:, :hd], rows[:, hd:2 * hd], rows[:, 2 * hd:]
    n_keep = min(TOPK_KEYS, lk // 4)
    kpos = jnp.arange(lk, dtype=jnp.int32)
    scale = hd ** -0.5

    def blk(start, qpos, qb, qib, wb):
        qf = qpos[:, None]
        rel = jax.nn.relu(jnp.einsum('qhd,kd->qhk', qib, k_idx).astype(jnp.float32))
        isc = jnp.einsum('qh,qhk->qk', wb.astype(jnp.float32), rel)
        isc = jnp.where(kpos[None, :] <= qf, isc, -jnp.inf)
        top_s, top_i = lax.top_k(isc, n_keep)
        s = jnp.einsum('qhd,qkd->hqk', qb, k[top_i]).astype(jnp.float32) * scale + t5_bias(table, qf - top_i)
        pr = masked_softmax(s, (top_s > -jnp.inf)[None])
        return jnp.einsum('hqk,qkd->qhd', pr.astype(v.dtype), v[top_i]).reshape(qb.shape[0], H_DSA * hd)

    return sweep(blk, P, q, q_idx, w_idx)


def attend_seq(qs, rows, past, P, w_len, p, table):
    mla_q, sb_q, nsa_q, nsa_g, dsa_q, q_idx, w_idx = qs
    mla_r, sb_r, nsa_r, win_r, dsa_r = [jnp.concatenate([a, b], axis=0) for a, b in zip(past, rows)]
    win_r = jnp.pad(win_r, ((WINDOW - w_len, 0), (0, 0)))
    o_mla = mla_attend(mla_q, mla_r, P, p['w_mla_ukv'], p['g_mla_k'][:DN_MLA])
    o_sb = sb_attend(sb_q, sb_r, P)
    o_nsa = nsa_attend(nsa_q, nsa_g, nsa_r, win_r, P, w_len, p, table[:, :H_NSA])
    o_dsa = dsa_attend(dsa_q, q_idx, w_idx, dsa_r, P, table[:, H_NSA:])
    return (o_mla, o_sb, o_nsa, o_dsa)


def bias_tiles(table):
    r = jnp.arange(TILE, dtype=jnp.int32)
    tiles = [t5_bias(table, d * TILE + r[:, None] - r[None, :]) for d in (0, 1)]
    far = t5_bias(table, jnp.full((TILE, TILE), 2 * TILE + T5_MAX_DIST, jnp.int32))
    return jnp.stack(tiles + [far], axis=0)


def selection_matrix(ncp, n_cmp, n_slc, nsp):
    ratio = SLC_BLOCK // CMP_STRIDE
    span = CMP_BLOCK // CMP_STRIDE
    s = np.zeros((ncp, nsp), np.float32)
    for b in range(n_slc):
        for m in range(ratio):
            for n in range(span):
                c = ratio * b + m + n - (span - 1)
                if 0 <= c < n_cmp:
                    s[c, b] += 1.0
    return jnp.asarray(s, MXU_DTYPE)


def prompt_mixers(qs, rows, p, tb_nsa, tb_dsa, bias_c, smat):
    mla_q, sb_q, nsa_q, nsa_g, dsa_q, q_idx, w_idx = qs
    mla_r, sb_r, nsa_r, win_r, dsa_r = rows
    B, L = mla_r.shape[:2]
    hd = HEAD_DIM
    heads_first = lambda x: jnp.transpose(x, (0, 2, 1, 3))
    kv = lin(mla_r[..., :KV_LORA].reshape(B * L, KV_LORA), p['w_mla_ukv'], tm=min(512, B * L))
    kv = kv.reshape(B, L, H_MLA, DN_MLA + DV_MLA)
    k_n = rmsnorm(kv[..., :DN_MLA], p['g_mla_k'][:DN_MLA])
    k_r = jnp.broadcast_to(mla_r[:, :, None, KV_LORA:], (B, L, H_MLA, DR_MLA))
    zpad = jnp.zeros((B, L, H_MLA, LANES - DN_MLA - DR_MLA), F32)
    k_pack = _mx(jnp.concatenate([k_n, k_r, zpad], axis=-1)).reshape(B, L, H_MLA * LANES)
    q_pack = _mx(jnp.concatenate([mla_q, zpad], axis=-1)).reshape(B, L, H_MLA * LANES)
    v_pack = _mx(kv[..., DN_MLA:]).reshape(B, L, H_MLA * DV_MLA)
    o_mla = mla_prompt(q_pack, k_pack, v_pack)
    o_sb = sb_prompt(heads_first(sb_q), sb_r[..., :hd], sb_r[..., hd:])
    n_chunk = L // CMP_STRIDE
    n_cmp = n_chunk - 1
    ncp = smat.shape[0]
    raw = jnp.stack([nsa_r[..., :hd], nsa_r[..., hd:2 * hd]], axis=1).reshape(B, 2, n_chunk, CMP_STRIDE * hd)
    blocks = jnp.concatenate([raw[:, :, :-1], raw[:, :, 1:]], axis=-1) + p['pe_cmp'].reshape(1, 2, 1, CMP_BLOCK * hd)
    blocks = jnp.pad(blocks, ((0, 0), (0, 0), (0, ncp - n_cmp), (0, 0)))
    cmp_kv = compress(blocks, p['w_cmp1'], p['w_cmp2'], p['g_nsa_k'][0])
    o_nsa = nsa_prompt(heads_first(nsa_q), heads_first(nsa_g), cmp_kv, bias_c,
                       nsa_r[..., 2 * hd:3 * hd], nsa_r[..., 3 * hd:], win_r[..., :hd], win_r[..., hd:], smat, tb_nsa)
    o_dsa = dsa_prompt(heads_first(dsa_q), heads_first(q_idx), w_idx, dsa_r[..., :hd], dsa_r[..., hd:2 * hd],
                       dsa_r[..., 2 * hd:], tb_dsa)
    return tuple(o.reshape(B * L, BR_WIDTH) for o in (o_mla, o_sb, o_nsa, o_dsa))


def pack_w_in(w):
    cuts = np.cumsum((0,) + IN_SIZES)
    cols = []
    for i in range(10):
        cols.append(jnp.pad(w[:, cuts[i]:cuts[i + 1]], ((0, 0), (0, _SEG_PAD[i] - IN_SIZES[i]))))
    tail = w[:, cuts[10]:cuts[12]]
    cols.append(jnp.pad(tail, ((0, 0), (0, N_SMALL - C_DIK - tail.shape[1]))))
    cols.append(w[:, cuts[12]:])
    return _mx(jnp.concatenate(cols, axis=1))


def kernel(x_prompt, x_sample, cache_mla, cache_sb, cache_nsa, cache_dsa, state_nsa_win, page_table,
           c_prompt, c_sample, w_ada, b_ada, g_norm_mix, g_norm_ffn, w_in, g_mla_qa, w_mla_uq, g_mla_kva,
           w_mla_ukv, g_mla_q, g_mla_k, w_cmp1, w_cmp2, pe_cmp, g_nsa_q, g_nsa_k, g_dsa_q, g_dsa_k, rel_bias,
           w_br, w_out, w_router, b_router, w_gu, b_gu, w_down, b_down):
    B, L, D = x_prompt.shape
    S, seq_s, _ = x_sample.shape
    past_len = page_table.shape[1] * cache_mla.shape[2]
    w_len = state_nsa_win.shape[2]
    w_len_p = min(WINDOW, L)
    pos_p = jnp.arange(L, dtype=jnp.int32)
    pos_s = past_len + jnp.arange(seq_s, dtype=jnp.int32)
    tm_p = min(256, L)
    tpg_p = L // tm_p
    ts = S * seq_s

    tb_nsa = bias_tiles(rel_bias[:, :H_NSA])
    tb_dsa = bias_tiles(rel_bias[:, H_NSA:])
    n_cmp = L // CMP_STRIDE - 1
    ncp = -(-n_cmp // LANES) * LANES
    n_slc = -(-L // SLC_BLOCK)
    nsp = -(-n_slc // LANES) * LANES
    cmp_end = jnp.arange(ncp, dtype=jnp.int32) * CMP_STRIDE + (CMP_BLOCK - 1)
    bias_c = t5_bias(rel_bias[:, :H_NSA], pos_p[:, None] - cmp_end[None, :])
    smat = selection_matrix(ncp, n_cmp, n_slc, nsp)

    xp = x_prompt.reshape(B * L, D)
    xs = x_sample.reshape(ts, D)
    c_all = jnp.concatenate([c_prompt, c_sample], axis=0)
    out_p = []
    out_s = []
    for l in range(DEPTH):
        p = {'g_mla_qa': g_mla_qa[l], 'w_mla_uq': w_mla_uq[l], 'g_mla_kva': g_mla_kva[l],
             'w_mla_ukv': w_mla_ukv[l], 'g_mla_q': g_mla_q[l], 'g_mla_k': g_mla_k[l], 'w_cmp1': w_cmp1[l],
             'w_cmp2': w_cmp2[l], 'pe_cmp': pe_cmp[l], 'g_nsa_q': g_nsa_q[l], 'g_nsa_k': g_nsa_k[l],
             'g_dsa_q': g_dsa_q[l], 'g_dsa_k': g_dsa_k[l]}
        mod = lin(c_all, w_ada[l], b=b_ada[l], pro="silu", tn=6 * D // 4)
        mod_p = [m.reshape(B, 1, D) for m in jnp.split(mod[:B], 6, axis=-1)]
        mod_s = [m.reshape(1, ts, D) for m in jnp.split(mod[B:], 6, axis=-1)]
        w_p = pack_w_in(w_in[l])

        y_p, gates_p = proj_in(xp, g_norm_mix[l], mod_p[0], mod_p[1], w_p, tm_p, tpg_p)
        qs_p, rows_p = group_transforms(y_p.reshape(B, L, N_SMALL), pos_p, p)
        o4_p = prompt_mixers(qs_p, rows_p, p, tb_nsa, tb_dsa, bias_c, smat)
        xp = merge(o4_p, gates_p, w_br[l], w_out[l], xp, mod_p[2], tm_p, tpg_p)

        y_s, gates_s = proj_in(xs, g_norm_mix[l], mod_s[0], mod_s[1], w_p, ts, 1)
        qs_s, rows_s = group_transforms(y_s.reshape(S, seq_s, N_SMALL), pos_s, p)

        def sample_one(args):
            qs, rows, prow, wbuf = args
            past = (cache_mla[l, prow].reshape(-1, MLA_KV_W), cache_sb[l, prow].reshape(-1, SB_KV_W),
                    cache_nsa[l, prow].reshape(-1, NSA_KV_W), wbuf, cache_dsa[l, prow].reshape(-1, DSA_KV_W))
            return attend_seq(qs, rows, past, past_len, w_len, p, rel_bias)

        o4_s = lax.map(sample_one, (qs_s, rows_s, page_table, state_nsa_win[l]))
        o4_s = tuple(o.reshape(ts, BR_WIDTH) for o in o4_s)
        xs = merge(o4_s, gates_s, w_br[l], w_out[l], xs, mod_s[2], ts, 1)

        h_p, lg_p = router(xp, g_norm_ffn[l], mod_p[3], mod_p[4], w_router[l], b_router[l], tm_p, tpg_p)
        h_s, lg_s = router(xs, g_norm_ffn[l], mod_s[3], mod_s[4], w_router[l], b_router[l], ts, 1)
        wg = _mx(jnp.concatenate([w_gu[l][:, :, 0::2], w_gu[l][:, :, 1::2]], axis=-1))
        bg = jnp.concatenate([b_gu[l][:, 0::2], b_gu[l][:, 1::2]], axis=-1)[:, None, :]
        y_all = moe_ffn(jnp.concatenate([h_p, h_s], axis=0),
                        jnp.concatenate([lg_p, lg_s], axis=0)[:, :N_EXPERTS],
                        wg, bg, _mx(w_down[l]), b_down[l][:, None, :])
        xp = xp + jnp.broadcast_to(mod_p[5], (B, L, D)).reshape(B * L, D) * y_all[:B * L]
        xs = xs + mod_s[5].reshape(ts, D) * y_all[B * L:]

        out_p.append((rows_p[0], rows_p[1], rows_p[2], rows_p[4], rows_p[3][:, L - w_len_p:]))
        win_s = jnp.concatenate([state_nsa_win[l], rows_s[3]], axis=1)
        out_s.append((rows_s[0], rows_s[1], rows_s[2], rows_s[4], win_s[:, win_s.shape[1] - w_len:]))
    new_p = [jnp.stack([o[i] for o in out_p], axis=0) for i in range(5)]
    new_s = [jnp.stack([o[i] for o in out_s], axis=0) for i in range(5)]
    return (xp.reshape(B, L, D), xs.reshape(S, seq_s, D), *new_p, *new_s)
```

```python
import functools
import math
import jax, jax.numpy as jnp
from jax import lax
import numpy as np
from jax.experimental import pallas as pl
from jax.experimental.pallas import tpu as pltpu

D_MODEL = 1024
DEPTH = 2
HEAD_DIM = 64
H_MLA = 4
Q_LORA = 256
KV_LORA = 128
DN_MLA = 64
DR_MLA = 32
DV_MLA = 64
ROPE_THETA = 10000.0
H_SB = 4
H_NSA = 4
CMP_STRIDE = 16
CMP_BLOCK = 2 * CMP_STRIDE
CMP_HIDDEN = 128
SLC_BLOCK = 64
N_SELECT = 16
N_LOCAL = 2
WINDOW = 512
H_DSA = 4
H_IDX = 4
D_IDX = 32
TOPK_KEYS = 256
NUM_BUCKETS = 32
T5_MAX_DIST = 128
N_EXPERTS = 32
TOP_K = 4
D_FF = 512
SWIGLU_LIMIT = 7.0
SWIGLU_ALPHA = 1.702
MOE_BLOCK = 128
QBLOCK = 128
N_BRANCH = 4
BR_WIDTH = 256
EPS = 1e-6
MLA_KV_W = KV_LORA + DR_MLA
SB_KV_W = 2 * HEAD_DIM
NSA_KV_W = 4 * HEAD_DIM
WIN_KV_W = 2 * HEAD_DIM
DSA_KV_W = 2 * HEAD_DIM + D_IDX
IN_SIZES = (Q_LORA, MLA_KV_W, H_SB * HEAD_DIM, SB_KV_W, H_NSA * HEAD_DIM, NSA_KV_W + WIN_KV_W, 3 * H_NSA,
            H_DSA * HEAD_DIM, 2 * HEAD_DIM, H_IDX * D_IDX, D_IDX, H_IDX, N_BRANCH * D_MODEL)

MXU_DTYPE = jnp.bfloat16
F32 = jnp.float32
NEG_INF = float("-inf")
LANES = 128
TILE = LANES
VMEM_LIMIT_BYTES = 56 * 1024 * 1024
MOE_ROWS = 256
INT_MIN = -2 ** 31

_SEG_PAD = (256, 256, 256, 128, 256, 384, 128, 256, 128, 128, 0, 0)
C_MQA, C_MKV, C_SQ, C_SKV, C_NQ, C_NKV, C_NG, C_DQ, C_DKV, C_DIQ, C_DIK = (
    0, 256, 512, 768, 896, 1152, 1536, 1664, 1920, 2048, 2176)
C_DIW = C_DIK + D_IDX
N_SMALL = 2304


def _mx(a):
    return a.astype(MXU_DTYPE)


def _dot(a, b):
    return jnp.dot(_mx(a), _mx(b), preferred_element_type=F32)


def _dot_nt(a, b):
    return lax.dot_general(_mx(a), _mx(b), (((1,), (1,)), ((), ())), preferred_element_type=F32)


def _exact_dot(a, b, left=True):
    b = b.astype(MXU_DTYPE)
    f = (lambda u: jnp.dot(u, b, preferred_element_type=F32)) if left else (
        lambda u: jnp.dot(b, u, preferred_element_type=F32))
    if MXU_DTYPE == F32:
        return f(a)
    hi = a.astype(MXU_DTYPE)
    r1 = a - hi.astype(F32)
    mid = r1.astype(MXU_DTYPE)
    lo = (r1 - mid.astype(F32)).astype(MXU_DTYPE)
    return f(hi) + f(mid) + f(lo)


def _cparams(*sem):
    return pltpu.CompilerParams(dimension_semantics=sem, vmem_limit_bytes=VMEM_LIMIT_BYTES)


def _adaln_rows(x, g, sh, sc):
    y = x * lax.rsqrt(jnp.mean(x * x, axis=-1, keepdims=True) + EPS)
    return y * g * (1.0 + sc) + sh


def _proj_in_kernel(x_ref, g_ref, sh_ref, sc_ref, w_ref, ys_ref, gate_ref):
    h = _mx(_adaln_rows(x_ref[...], g_ref[...], sh_ref[...], sc_ref[...]))
    ys_ref[...] = jnp.dot(h, w_ref[:, :N_SMALL], preferred_element_type=F32)
    for c in range(N_BRANCH):
        lo = N_SMALL + c * D_MODEL
        gate_ref[:, c * D_MODEL:(c + 1) * D_MODEL] = jax.nn.sigmoid(
            jnp.dot(h, w_ref[:, lo:lo + D_MODEL], preferred_element_type=F32))


def _mod_specs(mod, tm, tiles_per_group):
    r = mod.shape[1]
    return pl.BlockSpec((None, r, mod.shape[2]), lambda i: (i // tiles_per_group, 0, 0))


def proj_in(x2d, g, sh, sc, w_p, tm, tiles_per_group):
    t, d = x2d.shape
    n = w_p.shape[1]
    return pl.pallas_call(
        _proj_in_kernel,
        grid=(t // tm,),
        in_specs=[pl.BlockSpec((tm, d), lambda i: (i, 0)),
                  pl.BlockSpec((1, d), lambda i: (0, 0)),
                  _mod_specs(sh, tm, tiles_per_group), _mod_specs(sc, tm, tiles_per_group),
                  pl.BlockSpec((d, n), lambda i: (0, 0))],
        out_specs=[pl.BlockSpec((tm, N_SMALL), lambda i: (i, 0)),
                   pl.BlockSpec((tm, N_BRANCH * D_MODEL), lambda i: (i, 0))],
        out_shape=[jax.ShapeDtypeStruct((t, N_SMALL), F32), jax.ShapeDtypeStruct((t, N_BRANCH * D_MODEL), F32)],
        compiler_params=_cparams("parallel"),
        name="proj_in",
    )(x2d, g.reshape(1, d), sh, sc, w_p)


def _lin_kernel(*refs, pro, has_g, has_b):
    refs = list(refs)
    x_ref = refs.pop(0)
    g_ref = refs.pop(0) if has_g else None
    w_ref = refs.pop(0)
    b_ref = refs.pop(0) if has_b else None
    o_ref = refs.pop(0)
    x = x_ref[...]
    if pro == "silu":
        x = x * jax.nn.sigmoid(x)
    elif pro == "rms":
        x = x * lax.rsqrt(jnp.mean(x * x, axis=-1, keepdims=True) + EPS) * g_ref[...]
    y = _dot(x, w_ref[...])
    if has_b:
        y = y + b_ref[...]
    o_ref[...] = y.astype(o_ref.dtype)


def lin(x, w, g=None, b=None, pro="none", tm=512, tn=None, out_dtype=F32):
    m, k = x.shape
    n = w.shape[1]
    tm = min(tm, m)
    tn = n if tn is None else tn
    args = [x]
    specs = [pl.BlockSpec((tm, k), lambda i, j: (i, 0))]
    if g is not None:
        args.append(g.reshape(1, k))
        specs.append(pl.BlockSpec((1, k), lambda i, j: (0, 0)))
    args.append(_mx(w))
    specs.append(pl.BlockSpec((k, tn), lambda i, j: (0, j)))
    if b is not None:
        args.append(b.reshape(1, n))
        specs.append(pl.BlockSpec((1, tn), lambda i, j: (0, j)))
    return pl.pallas_call(
        functools.partial(_lin_kernel, pro=pro, has_g=g is not None, has_b=b is not None),
        grid=(m // tm, n // tn),
        in_specs=specs,
        out_specs=pl.BlockSpec((tm, tn), lambda i, j: (i, j)),
        out_shape=jax.ShapeDtypeStruct((m, n), out_dtype),
        compiler_params=_cparams("parallel", "parallel"),
        name="lin_" + pro,
    )(*args)


def _router_kernel(x_ref, g_ref, sh_ref, sc_ref, w_ref, b_ref, h_ref, lg_ref):
    h = _adaln_rows(x_ref[...], g_ref[...], sh_ref[...], sc_ref[...])
    h_ref[...] = h.astype(h_ref.dtype)
    lg_ref[...] = _dot(h, w_ref[...]) + b_ref[...]


def router(x2d, g, sh, sc, w_r, b_r, tm, tiles_per_group):
    t, d = x2d.shape
    wp = jnp.pad(_mx(w_r), ((0, 0), (0, LANES - N_EXPERTS)))
    bp = jnp.pad(b_r.reshape(1, -1), ((0, 0), (0, LANES - N_EXPERTS)), constant_values=NEG_INF)
    return pl.pallas_call(
        _router_kernel,
        grid=(t // tm,),
        in_specs=[pl.BlockSpec((tm, d), lambda i: (i, 0)),
                  pl.BlockSpec((1, d), lambda i: (0, 0)),
                  _mod_specs(sh, tm, tiles_per_group), _mod_specs(sc, tm, tiles_per_group),
                  pl.BlockSpec((d, LANES), lambda i: (0, 0)),
                  pl.BlockSpec((1, LANES), lambda i: (0, 0))],
        out_specs=[pl.BlockSpec((tm, d), lambda i: (i, 0)), pl.BlockSpec((tm, LANES), lambda i: (i, 0))],
        out_shape=[jax.ShapeDtypeStruct((t, d), MXU_DTYPE), jax.ShapeDtypeStruct((t, LANES), F32)],
        compiler_params=_cparams("parallel"),
        name="router",
    )(x2d, g.reshape(1, d), sh, sc, wp, bp)


def _moe_kernel(be_ref, x_ref, wg_ref, bg_ref, wd_ref, bd_ref, o_ref):
    gu = _dot(x_ref[...], wg_ref[...]) + bg_ref[...]
    glu = jnp.minimum(gu[:, :D_FF], SWIGLU_LIMIT)
    lin_ = jnp.clip(gu[:, D_FF:], -SWIGLU_LIMIT, SWIGLU_LIMIT)
    act = glu * jax.nn.sigmoid(SWIGLU_ALPHA * glu) * (lin_ + 1.0)
    o_ref[...] = _dot(act, wd_ref[...]) + bd_ref[...]


def moe_blocks(xg, blk_e, wg, bg, wd, bd):
    n, d = xg.shape
    nb = n // MOE_ROWS
    grid_spec = pltpu.PrefetchScalarGridSpec(
        num_scalar_prefetch=1,
        grid=(nb,),
        in_specs=[pl.BlockSpec((MOE_ROWS, d), lambda b, be: (b, 0)),
                  pl.BlockSpec((None, d, 2 * D_FF), lambda b, be: (be[b], 0, 0)),
                  pl.BlockSpec((None, 1, 2 * D_FF), lambda b, be: (be[b], 0, 0)),
                  pl.BlockSpec((None, D_FF, d), lambda b, be: (be[b], 0, 0)),
                  pl.BlockSpec((None, 1, d), lambda b, be: (be[b], 0, 0))],
        out_specs=pl.BlockSpec((MOE_ROWS, d), lambda b, be: (b, 0)))
    return pl.pallas_call(
        _moe_kernel, grid_spec=grid_spec,
        out_shape=jax.ShapeDtypeStruct((n, d), F32),
        compiler_params=_cparams("arbitrary"),
        name="moe_ffn",
    )(blk_e, xg, wg, bg, wd, bd)


def moe_ffn(h, logits, wg, bg, wd, bd):
    t, d = h.shape
    top_v, top_e = lax.top_k(logits, TOP_K)
    wts = jax.nn.softmax(top_v, axis=-1)
    a = t * TOP_K
    flat_e = top_e.reshape(-1)
    onehot = (flat_e[:, None] == jnp.arange(N_EXPERTS, dtype=jnp.int32)[None, :]).astype(jnp.int32)
    rank = jnp.sum((jnp.cumsum(onehot, axis=0) - onehot) * onehot, axis=1)
    counts = jnp.sum(onehot, axis=0)
    padded = (counts + MOE_ROWS - 1) // MOE_ROWS * MOE_ROWS
    pad_end = jnp.cumsum(padded)
    pad_start = pad_end - padded
    dest = (pad_start[flat_e] + rank).astype(jnp.int32)
    nb = -(-a // MOE_ROWS) + N_EXPERTS
    slot_tok = jnp.full((nb * MOE_ROWS,), t, jnp.int32).at[dest].set(jnp.arange(a, dtype=jnp.int32) // TOP_K)
    blk_e = jnp.minimum(jnp.searchsorted(pad_end, jnp.arange(nb, dtype=jnp.int32) * MOE_ROWS, side='right'),
                        N_EXPERTS - 1).astype(jnp.int32)
    hpad = jnp.concatenate([h, jnp.zeros((1, d), h.dtype)], axis=0)
    out = moe_blocks(hpad[slot_tok], blk_e, wg, bg, wd, bd)
    return jnp.einsum('tk,tkd->td', wts, out[dest].reshape(t, TOP_K, d))


def _merge_kernel(o0_ref, o1_ref, o2_ref, o3_ref, gates_ref, wbr_ref, wout_ref, x_ref, gm_ref, out_ref):
    acc = None
    for b, o_ref in enumerate((o0_ref, o1_ref, o2_ref, o3_ref)):
        t = gates_ref[:, b * D_MODEL:(b + 1) * D_MODEL] * _dot(o_ref[...], wbr_ref[b])
        acc = t if acc is None else acc + t
    out_ref[...] = x_ref[...] + gm_ref[...] * _dot(acc, wout_ref[...])


def merge(o4, gates, w_br, w_out, x2d, gm, tm, tiles_per_group):
    t, d = x2d.shape
    ospec = pl.BlockSpec((tm, BR_WIDTH), lambda i: (i, 0))
    return pl.pallas_call(
        _merge_kernel,
        grid=(t // tm,),
        in_specs=[ospec, ospec, ospec, ospec,
                  pl.BlockSpec((tm, N_BRANCH * d), lambda i: (i, 0)),
                  pl.BlockSpec((N_BRANCH, BR_WIDTH, d), lambda i: (0, 0, 0)),
                  pl.BlockSpec((d, d), lambda i: (0, 0)),
                  pl.BlockSpec((tm, d), lambda i: (i, 0)),
                  _mod_specs(gm, tm, tiles_per_group)],
        out_specs=pl.BlockSpec((tm, d), lambda i: (i, 0)),
        out_shape=jax.ShapeDtypeStruct((t, d), F32),
        compiler_params=_cparams("parallel"),
        name="merge",
    )(*o4, gates, _mx(w_br), _mx(w_out), x2d, gm)


def _order_key(x):
    bits = lax.bitcast_convert_type(x, jnp.int32)
    return jnp.where(bits >= 0, bits, bits ^ jnp.int32(0x7FFFFFFF))


KEY_NEG_INF = int(np.array(-np.inf, np.float32).view(np.int32)) ^ 0x7FFFFFFF
KEY_NEG_INF = KEY_NEG_INF - 2 ** 32 if KEY_NEG_INF >= 2 ** 31 else KEY_NEG_INF


def _kth_threshold(count_ge, k, rows):
    zero = jnp.zeros((rows, 1), jnp.int32)
    t = jnp.where(count_ge(zero) >= k, zero, jnp.full((rows, 1), INT_MIN, jnp.int32))

    def step(s, t):
        cand = t + (jnp.int32(1) << (30 - s))
        return jnp.where(count_ge(cand) >= k, cand, t)

    return lax.fori_loop(0, 31, step, t)


def _lower_tri(n):
    return (lax.broadcasted_iota(jnp.int32, (n, n), 0) < lax.broadcasted_iota(jnp.int32, (n, n), 1)).astype(MXU_DTYPE)


def _topk_mask_row(score, k):
    r, c = score.shape
    key = _order_key(score)

    def count_ge(t):
        return jnp.sum(jnp.where(key >= t, 1.0, 0.0), axis=1, keepdims=True)

    t = _kth_threshold(count_ge, float(k), r)
    gt = key > t
    eq = key == t
    need = float(k) - jnp.sum(jnp.where(gt, 1.0, 0.0), axis=1, keepdims=True)
    prefix = jnp.dot(jnp.where(eq, 1.0, 0.0).astype(MXU_DTYPE), _lower_tri(c), preferred_element_type=F32)
    return (gt | (eq & (prefix < need))) & (key > KEY_NEG_INF)


def _softmax_step(s, ok, m, l, acc, v):
    s = jnp.where(ok, s, NEG_INF)
    m_new = jnp.maximum(m, jnp.max(s, axis=1, keepdims=True))
    m_safe = jnp.where(m_new == NEG_INF, 0.0, m_new)
    p = jnp.exp(s - m_safe)
    alpha = jnp.exp(m - m_safe)
    return m_new, alpha * l + jnp.sum(p, axis=1, keepdims=True), alpha * acc + _dot(p, v)


def _tile_bias(tb_ref, d, heads):
    b = jnp.where(d == 0, tb_ref[0], jnp.where(d == 1, tb_ref[1], tb_ref[2]))
    return b.reshape(heads * TILE, TILE)


def _store_heads(o_ref, o, heads, rows, width):
    for h in range(heads):
        o_ref[:, h * width:(h + 1) * width] = o[h * rows:(h + 1) * rows]


def _mla_kernel(q_ref, k_ref, v_ref, o_ref, m_sc, l_sc, acc_sc, *, tq, tk, scale):
    i = pl.program_id(1)
    j = pl.program_id(2)

    @pl.when(j == 0)
    def _():
        m_sc[...] = jnp.full(m_sc.shape, NEG_INF, F32)
        l_sc[...] = jnp.zeros(l_sc.shape, F32)
        acc_sc[...] = jnp.zeros(acc_sc.shape, F32)

    @pl.when(j * tk < (i + 1) * tq)
    def _():
        qpos = i * tq + lax.broadcasted_iota(jnp.int32, (tq, tk), 0)
        kpos = j * tk + lax.broadcasted_iota(jnp.int32, (tq, tk), 1)
        causal = kpos <= qpos
        for h in range(H_MLA):
            s = _dot_nt(q_ref[:, h * LANES:(h + 1) * LANES], k_ref[:, h * LANES:(h + 1) * LANES]) * scale
            m, l, acc = _softmax_step(s, causal, m_sc[h], l_sc[h], acc_sc[h], v_ref[:, h * DV_MLA:(h + 1) * DV_MLA])
            m_sc[h] = m
            l_sc[h] = l
            acc_sc[h] = acc

    @pl.when(j == pl.num_programs(2) - 1)
    def _():
        for h in range(H_MLA):
            o_ref[:, h * DV_MLA:(h + 1) * DV_MLA] = acc_sc[h] / l_sc[h]


def mla_prompt(q, k, v):
    b, l, _ = q.shape
    tq = tk = min(256, l)
    last = lambda i: ((i + 1) * tq - 1) // tk
    return pl.pallas_call(
        functools.partial(_mla_kernel, tq=tq, tk=tk, scale=(DN_MLA + DR_MLA) ** -0.5),
        grid=(b, l // tq, l // tk),
        in_specs=[pl.BlockSpec((None, tq, H_MLA * LANES), lambda bb, i, j: (bb, i, 0)),
                  pl.BlockSpec((None, tk, H_MLA * LANES), lambda bb, i, j: (bb, jnp.minimum(j, last(i)), 0)),
                  pl.BlockSpec((None, tk, H_MLA * DV_MLA), lambda bb, i, j: (bb, jnp.minimum(j, last(i)), 0))],
        out_specs=pl.BlockSpec((None, tq, H_MLA * DV_MLA), lambda bb, i, j: (bb, i, 0)),
        out_shape=jax.ShapeDtypeStruct((b, l, H_MLA * DV_MLA), F32),
        scratch_shapes=[pltpu.VMEM((H_MLA, tq, 1), F32), pltpu.VMEM((H_MLA, tq, 1), F32),
                        pltpu.VMEM((H_MLA, tq, DV_MLA), F32)],
        compiler_params=_cparams("parallel", "parallel", "arbitrary"),
        name="mla_prompt",
    )(q, k, v)


def _sb_kernel(q_ref, k_ref, v_ref, tri_ref, o_ref, carry_sc, acc_sc, *, tq, nsub):
    i = pl.program_id(1)
    s_id = pl.program_id(2)
    ks = nsub * TILE
    kb = ((i + 1) * tq - 1) // ks - s_id
    rows = H_SB * tq

    @pl.when(s_id == 0)
    def _():
        carry_sc[...] = jnp.zeros(carry_sc.shape, F32)
        acc_sc[...] = jnp.zeros(acc_sc.shape, F32)

    @pl.when(kb >= 0)
    def _():
        q = q_ref[...].reshape(rows, HEAD_DIM)
        qpos = i * tq + (lax.broadcasted_iota(jnp.int32, (rows, TILE), 0) & (tq - 1))
        lane = lax.broadcasted_iota(jnp.int32, (rows, TILE), 1)
        for u in reversed(range(nsub)):
            kstart = kb * ks + u * TILE

            @pl.when(kstart < (i + 1) * tq)
            def _():
                z = _dot_nt(q, k_ref[u]) * HEAD_DIM ** -0.5
                before = (kstart + lane) < qpos
                ls = jnp.minimum(z, 0.0) - jnp.log1p(jnp.exp(-jnp.abs(z)))
                lk = jnp.where(before, ls - z, 0.0)
                both = _exact_dot(lk, tri_ref[...])
                later = both[:, :TILE] + carry_sc[...]
                a = jnp.where(before, jnp.exp(ls + later), 0.0)
                acc_sc[...] += _dot(a, v_ref[u])
                carry_sc[...] += both[:, TILE:]

    @pl.when(s_id == pl.num_programs(2) - 1)
    def _():
        _store_heads(o_ref, acc_sc[...], H_SB, tq, HEAD_DIM)


def sb_prompt(q4, k, v):
    b, _, l, hd = q4.shape
    tq = TILE
    nsub = min(4, l // TILE)
    ks = nsub * TILE
    nt = l // ks
    idx = lax.broadcasted_iota(jnp.int32, (TILE, TILE), 0) > lax.broadcasted_iota(jnp.int32, (TILE, TILE), 1)
    tri = jnp.concatenate([idx.astype(MXU_DTYPE), jnp.ones((TILE, TILE), MXU_DTYPE)], axis=1)
    kmap = lambda bb, i, s: (bb, jnp.maximum(((i + 1) * tq - 1) // ks - s, 0), 0, 0)
    k4 = _mx(k).reshape(b, nt * nsub, TILE, hd)
    v4 = _mx(v).reshape(b, nt * nsub, TILE, hd)
    return pl.pallas_call(
        functools.partial(_sb_kernel, tq=tq, nsub=nsub),
        grid=(b, l // tq, nt),
        in_specs=[pl.BlockSpec((None, H_SB, tq, hd), lambda bb, i, s: (bb, 0, i, 0)),
                  pl.BlockSpec((None, nsub, TILE, hd), kmap),
                  pl.BlockSpec((None, nsub, TILE, hd), kmap),
                  pl.BlockSpec((TILE, 2 * TILE), lambda bb, i, s: (0, 0))],
        out_specs=pl.BlockSpec((None, tq, H_SB * hd), lambda bb, i, s: (bb, i, 0)),
        out_shape=jax.ShapeDtypeStruct((b, l, H_SB * hd), F32),
        scratch_shapes=[pltpu.VMEM((H_SB * tq, TILE), F32), pltpu.VMEM((H_SB * tq, hd), F32)],
        compiler_params=_cparams("parallel", "parallel", "arbitrary"),
        name="sb_prompt",
    )(_mx(q4), k4, v4, tri)


def _cmp_kernel(x_ref, w1_ref, w2_ref, g_ref, o_ref):
    y = _dot(jax.nn.gelu(_dot(x_ref[...], w1_ref[...])), w2_ref[...])
    yn = y * lax.rsqrt(jnp.mean(y * y, axis=-1, keepdims=True) + EPS) * g_ref[...]
    o_ref[...] = jnp.where(pl.program_id(1) == 0, yn, y)


def compress(blocks, w1, w2, g):
    b, _, nc, kdim = blocks.shape
    return pl.pallas_call(
        _cmp_kernel,
        grid=(b, 2),
        in_specs=[pl.BlockSpec((None, None, nc, kdim), lambda bb, i: (bb, i, 0, 0)),
                  pl.BlockSpec((None, kdim, CMP_HIDDEN), lambda bb, i: (i, 0, 0)),
                  pl.BlockSpec((None, CMP_HIDDEN, HEAD_DIM), lambda bb, i: (i, 0, 0)),
                  pl.BlockSpec((1, HEAD_DIM), lambda bb, i: (0, 0))],
        out_specs=pl.BlockSpec((None, None, nc, HEAD_DIM), lambda bb, i: (bb, i, 0, 0)),
        out_shape=jax.ShapeDtypeStruct((b, 2, nc, HEAD_DIM), F32),
        compiler_params=_cparams("parallel", "parallel"),
        name="nsa_compress",
    )(blocks, _mx(w1), _mx(w2), g.reshape(1, HEAD_DIM))


def _nsa_kernel(q_ref, gate_ref, kc_ref, vc_ref, bc_ref, ks_ref, vs_ref, kw_ref, vw_ref, smat_ref, tb_ref, o_ref,
                *, n_sel):
    i = pl.program_id(1)
    rows = H_NSA * TILE
    scale = HEAD_DIM ** -0.5
    ncp = kc_ref.shape[0]
    nsp = smat_ref.shape[1]
    q = q_ref[...].reshape(rows, HEAD_DIM)
    qp4 = i * TILE + (lax.broadcasted_iota(jnp.int32, (rows, 1), 0) & (TILE - 1))
    qp = i * TILE + lax.broadcasted_iota(jnp.int32, (TILE, 1), 0)
    lane = lax.broadcasted_iota(jnp.int32, (TILE, TILE), 1)
    rep4 = lambda x: jnp.concatenate([x] * H_NSA, axis=0)

    cmp_end = lax.broadcasted_iota(jnp.int32, (1, ncp), 1) * CMP_STRIDE + (CMP_BLOCK - 1)
    s = _dot_nt(q, kc_ref[...]) * scale + bc_ref[...].reshape(rows, ncp)
    s = jnp.where(qp4 >= cmp_end, s, NEG_INF)
    m = jnp.max(s, axis=1, keepdims=True)
    e = jnp.exp(s - jnp.where(m == NEG_INF, 0.0, m))
    d = jnp.sum(e, axis=1, keepdims=True)
    pc = e / jnp.where(d > 0, d, 1.0)
    o_c = _dot(pc, vc_ref[...])

    imp = ((pc[0:TILE] + pc[TILE:2 * TILE]) + pc[2 * TILE:3 * TILE]) + pc[3 * TILE:4 * TILE]
    score = _exact_dot(imp, smat_ref[...])
    blk = lax.broadcasted_iota(jnp.int32, (TILE, nsp), 1)
    back = qp // SLC_BLOCK - blk
    forced = (blk == 0) | ((back >= 0) & (back < N_LOCAL))
    score = jnp.where(back >= 0, jnp.where(forced, jnp.inf, score), NEG_INF)
    sel_f = jnp.where(_topk_mask_row(score, n_sel), 1.0, 0.0)

    zero_state = (jnp.full((rows, 1), NEG_INF, F32), jnp.zeros((rows, 1), F32), jnp.zeros((rows, HEAD_DIM), F32))
    blk_row = lax.broadcasted_iota(jnp.int32, (nsp, TILE), 0)
    half = lax.broadcasted_iota(jnp.int32, (nsp, TILE), 1) // SLC_BLOCK

    def slc_body(j, st):
        expand = jnp.where(blk_row == j * (TILE // SLC_BLOCK) + half, 1.0, 0.0)
        picked = _dot(sel_f, expand) > 0.5
        ok = picked & ((j * TILE + lane) <= qp)
        sc = _dot_nt(q, ks_ref[j]) * scale + _tile_bias(tb_ref, i - j, H_NSA)
        return _softmax_step(sc, rep4(ok), *st, vs_ref[j])

    _, l_s, acc_s = lax.fori_loop(0, i + 1, slc_body, zero_state)
    o_s = acc_s / jnp.where(l_s > 0, l_s, 1.0)

    def win_body(j, st):
        dist = qp - (j * TILE + lane)
        ok = (dist >= 0) & (dist < WINDOW)
        sc = _dot_nt(q, kw_ref[j]) * scale + _tile_bias(tb_ref, i - j, H_NSA)
        return _softmax_step(sc, rep4(ok), *st, vw_ref[j])

    _, l_w, acc_w = lax.fori_loop(jnp.maximum(i - WINDOW // TILE, 0), i + 1, win_body, zero_state)
    o_w = acc_w / jnp.where(l_w > 0, l_w, 1.0)

    g = gate_ref[...].reshape(rows, 3)
    o = g[:, 0:1] * o_c + g[:, 1:2] * o_s + g[:, 2:3] * o_w
    _store_heads(o_ref, o, H_NSA, TILE, HEAD_DIM)


def nsa_prompt(q4, gate4, cmp_kv, bias_c, k_slc, v_slc, k_win, v_win, smat, tb):
    b, _, l, hd = q4.shape
    nt = l // TILE
    ncp = cmp_kv.shape[2]
    nsp = smat.shape[1]
    n_sel = min(N_SELECT, l // SLC_BLOCK)
    tiles = lambda x: _mx(x).reshape(b, nt, TILE, hd)
    kv_spec = pl.BlockSpec((None, nt, TILE, hd), lambda bb, i: (bb, 0, 0, 0))
    return pl.pallas_call(
        functools.partial(_nsa_kernel, n_sel=n_sel),
        grid=(b, nt),
        in_specs=[pl.BlockSpec((None, H_NSA, TILE, hd), lambda bb, i: (bb, 0, i, 0)),
                  pl.BlockSpec((None, H_NSA, TILE, 3), lambda bb, i: (bb, 0, i, 0)),
                  pl.BlockSpec((None, None, ncp, hd), lambda bb, i: (bb, 0, 0, 0)),
                  pl.BlockSpec((None, None, ncp, hd), lambda bb, i: (bb, 1, 0, 0)),
                  pl.BlockSpec((H_NSA, TILE, ncp), lambda bb, i: (0, i, 0)),
                  kv_spec, kv_spec, kv_spec, kv_spec,
                  pl.BlockSpec((ncp, nsp), lambda bb, i: (0, 0)),
                  pl.BlockSpec((3, H_NSA, TILE, TILE), lambda bb, i: (0, 0, 0, 0))],
        out_specs=pl.BlockSpec((None, TILE, H_NSA * hd), lambda bb, i: (bb, i, 0)),
        out_shape=jax.ShapeDtypeStruct((b, l, H_NSA * hd), F32),
        compiler_params=_cparams("parallel", "parallel"),
        name="nsa_prompt",
    )(_mx(q4), gate4, _mx(cmp_kv), _mx(cmp_kv), bias_c, tiles(k_slc), tiles(v_slc), tiles(k_win), tiles(v_win),
      smat, tb)


def _dsa_kernel(q_ref, qi_ref, w_ref, kd_ref, vd_ref, ki_ref, tb_ref, o_ref, key_sc, *, n_keep):
    i = pl.program_id(1)
    rows = H_DSA * TILE
    scale = HEAD_DIM ** -0.5
    q = q_ref[...].reshape(rows, HEAD_DIM)
    qp = i * TILE + lax.broadcasted_iota(jnp.int32, (TILE, 1), 0)
    lane = lax.broadcasted_iota(jnp.int32, (TILE, TILE), 1)
    w = w_ref[...]

    def index_body(j, c):
        ki = ki_ref[j]
        isc = None
        for h in range(H_IDX):
            t = w[:, h:h + 1] * jnp.maximum(_dot_nt(qi_ref[h], ki), 0.0)
            isc = t if isc is None else isc + t
        isc = jnp.where((j * TILE + lane) <= qp, isc, NEG_INF)
        key_sc[j] = _order_key(isc)
        return c

    lax.fori_loop(0, i + 1, index_body, 0)

    def count_ge(t):
        def body(j, c):
            return c + jnp.where(key_sc[j] >= t, 1.0, 0.0)
        return jnp.sum(lax.fori_loop(0, i + 1, body, jnp.zeros((TILE, TILE), F32)), axis=1, keepdims=True)

    t = _kth_threshold(count_ge, float(n_keep), TILE)
    need = float(n_keep) - count_ge(t + 1)
    tri = _lower_tri(TILE)
    rep4 = lambda x: jnp.concatenate([x] * H_DSA, axis=0)

    def att_body(j, st):
        m, l, acc, seen = st
        key = key_sc[j]
        eq = key == t
        eq_f = jnp.where(eq, 1.0, 0.0)
        prefix = _dot(eq_f, tri) + seen
        ok = ((key > t) | (eq & (prefix < need))) & (key > KEY_NEG_INF)
        sc = _dot_nt(q, kd_ref[j]) * scale + _tile_bias(tb_ref, i - j, H_DSA)
        m, l, acc = _softmax_step(sc, rep4(ok), m, l, acc, vd_ref[j])
        return m, l, acc, seen + jnp.sum(eq_f, axis=1, keepdims=True)

    st0 = (jnp.full((rows, 1), NEG_INF, F32), jnp.zeros((rows, 1), F32), jnp.zeros((rows, HEAD_DIM), F32),
           jnp.zeros((TILE, 1), F32))
    _, l, acc, _ = lax.fori_loop(0, i + 1, att_body, st0)
    _store_heads(o_ref, acc / jnp.where(l > 0, l, 1.0), H_DSA, TILE, HEAD_DIM)


def dsa_prompt(q4, qi4, w_idx, k, v, k_idx, tb):
    b, _, l, hd = q4.shape
    nt = l // TILE
    n_keep = min(TOPK_KEYS, l // 4)
    tiles = lambda x: _mx(x).reshape(b, nt, TILE, x.shape[-1])
    kv_spec = lambda wd: pl.BlockSpec((None, nt, TILE, wd), lambda bb, i: (bb, 0, 0, 0))
    return pl.pallas_call(
        functools.partial(_dsa_kernel, n_keep=n_keep),
        grid=(b, nt),
        in_specs=[pl.BlockSpec((None, H_DSA, TILE, hd), lambda bb, i: (bb, 0, i, 0)),
                  pl.BlockSpec((None, H_IDX, TILE, D_IDX), lambda bb, i: (bb, 0, i, 0)),
                  pl.BlockSpec((None, TILE, H_IDX), lambda bb, i: (bb, i, 0)),
                  kv_spec(hd), kv_spec(hd), kv_spec(D_IDX),
                  pl.BlockSpec((3, H_DSA, TILE, TILE), lambda bb, i: (0, 0, 0, 0))],
        out_specs=pl.BlockSpec((None, TILE, H_DSA * hd), lambda bb, i: (bb, i, 0)),
        out_shape=jax.ShapeDtypeStruct((b, l, H_DSA * hd), F32),
        scratch_shapes=[pltpu.VMEM((nt, TILE, TILE), jnp.int32)],
        compiler_params=_cparams("parallel", "arbitrary"),
        name="dsa_prompt",
    )(_mx(q4), _mx(qi4), w_idx, tiles(k), tiles(v), tiles(k_idx), tb)


PAGES_PER_STEP = 16
QROWS = 8


def _page_specs(width, layer, npp, chunk_of):
    def spec(n):
        return pl.BlockSpec((None, None, PAGE_ROWS, width),
                            lambda s, c, pt: (layer, pt[s, chunk_of(c) * npp + n], 0, 0))
    return [spec(n) for n in range(npp)]


PAGE_ROWS = 128


def _lanes(x, times):
    return x if times == 1 else jnp.concatenate([x] * times, axis=1)


def _decode_softmax_step(s, ok, m, l):
    s = jnp.where(ok, s, NEG_INF)
    m_new = jnp.maximum(m, jnp.max(s, axis=1, keepdims=True))
    m_safe = jnp.where(m_new == NEG_INF, 0.0, m_new)
    p = jnp.exp(s - m_safe)
    alpha = jnp.exp(m - m_safe)
    return m_new, alpha * l + jnp.sum(p, axis=1, keepdims=True), p, alpha


def _mla_s_kernel(pt_ref, qg_ref, qr_ref, wt_ref, new_ref, *rest, npp, sub):
    pages = rest[:npp]
    o_ref, m_sc, l_sc, acc_sc = rest[npp:]
    c = pl.program_id(1)
    scale = (DN_MLA + DR_MLA) ** -0.5
    hw = DN_MLA + DV_MLA

    @pl.when(c == 0)
    def _():
        m_sc[...] = jnp.full(m_sc.shape, NEG_INF, F32)
        l_sc[...] = jnp.zeros(l_sc.shape, F32)
        acc_sc[...] = jnp.zeros(acc_sc.shape, F32)

    w_ext = jnp.concatenate([wt_ref[...], _mx(qr_ref[...])], axis=0)

    def process(rows, ok):
        n = rows.shape[0]
        kvt = lax.dot_general(w_ext, rows, (((1,), (1,)), ((), ())), preferred_element_type=F32)
        s_rows = []
        for h in range(H_MLA):
            kk = kvt[h * hw:h * hw + DN_MLA]
            ss = jnp.sum(kk * kk, axis=0, keepdims=True)
            a = jnp.sum(kk * _lanes(qg_ref[h * DN_MLA:(h + 1) * DN_MLA, :], n // LANES), axis=0, keepdims=True)
            s_rows.append(a * lax.rsqrt(ss / DN_MLA + EPS))
        s = jnp.concatenate(s_rows + [jnp.zeros((QROWS - H_MLA, n), F32)], axis=0)
        s = (s + kvt[H_MLA * hw:H_MLA * hw + QROWS]) * scale
        m, l, p, alpha = _decode_softmax_step(s, ok, m_sc[...], l_sc[...])
        m_sc[...] = m
        l_sc[...] = l
        for h in range(H_MLA):
            pv = kvt[h * hw + DN_MLA:(h + 1) * hw] * p[h:h + 1, :]
            part = pv[:, 0:LANES]
            for t in range(1, n // LANES):
                part = part + pv[:, t * LANES:(t + 1) * LANES]
            acc_sc[h] = alpha[h:h + 1, :] * acc_sc[h] + part

    for g in range(npp // sub):
        process(jnp.concatenate([_mx(pg[...]) for pg in pages[g * sub:(g + 1) * sub]], axis=0), True)

    @pl.when(c == pl.num_programs(1) - 1)
    def _():
        process(_mx(new_ref[...]), lax.broadcasted_iota(jnp.int32, (1, LANES), 1) == 0)
        l = l_sc[...]
        for h in range(H_MLA):
            o = jnp.sum(acc_sc[h], axis=1, keepdims=True) / l[h:h + 1, :]
            o_ref[h * DV_MLA:(h + 1) * DV_MLA, :] = jnp.broadcast_to(o, (DV_MLA, LANES))


def mla_sample(cache, layer, page_table, mla_q, new_rows, w_ukv, g_kn):
    s_n, n_pages = page_table.shape
    npp = min(PAGES_PER_STEP, n_pages)
    hw = DN_MLA + DV_MLA
    qg = (mla_q[..., :DN_MLA] * g_kn).reshape(s_n, H_MLA * DN_MLA, 1)
    qg = jnp.broadcast_to(qg, (s_n, H_MLA * DN_MLA, LANES))
    qr = jnp.pad(mla_q[..., DN_MLA:], ((0, 0), (0, QROWS - H_MLA), (KV_LORA, 0)))
    wt = _mx(jnp.pad(w_ukv.T, ((0, 0), (0, DR_MLA))))
    new_pad = jnp.pad(new_rows[:, None, :], ((0, 0), (0, LANES - 1), (0, 0)))
    grid_spec = pltpu.PrefetchScalarGridSpec(
        num_scalar_prefetch=1, grid=(s_n, n_pages // npp),
        in_specs=[pl.BlockSpec((None, H_MLA * DN_MLA, LANES), lambda s, c, pt: (s, 0, 0)),
                  pl.BlockSpec((None, QROWS, MLA_KV_W), lambda s, c, pt: (s, 0, 0)),
                  pl.BlockSpec((H_MLA * hw, MLA_KV_W), lambda s, c, pt: (0, 0)),
                  pl.BlockSpec((None, LANES, MLA_KV_W), lambda s, c, pt: (s, 0, 0))]
                 + _page_specs(MLA_KV_W, layer, npp, lambda c: c),
        out_specs=pl.BlockSpec((None, H_MLA * DV_MLA, LANES), lambda s, c, pt: (s, 0, 0)),
        scratch_shapes=[pltpu.VMEM((QROWS, 1), F32), pltpu.VMEM((QROWS, 1), F32),
                        pltpu.VMEM((H_MLA, DV_MLA, LANES), F32)])
    out = pl.pallas_call(
        functools.partial(_mla_s_kernel, npp=npp, sub=min(4, npp)), grid_spec=grid_spec,
        out_shape=jax.ShapeDtypeStruct((s_n, H_MLA * DV_MLA, LANES), F32),
        compiler_params=_cparams("parallel", "arbitrary"), name="mla_sample",
    )(page_table, qg, qr, wt, new_pad, *([cache] * npp))
    return out[:, :, 0]


def _sb_s_kernel(pt_ref, q_ref, tri_ref, t2_ref, *rest, npp):
    pages = rest[:npp]
    o_ref, carry_sc, acc_sc = rest[npp:]
    c = pl.program_id(1)
    nt = npp * PAGE_ROWS // LANES

    @pl.when(c == 0)
    def _():
        carry_sc[...] = jnp.zeros(carry_sc.shape, F32)
        acc_sc[...] = jnp.zeros(acc_sc.shape, F32)

    k = jnp.concatenate([_mx(pg[:, :HEAD_DIM]) for pg in pages], axis=0)
    v = jnp.concatenate([_mx(pg[:, HEAD_DIM:]) for pg in pages], axis=0)
    z8 = _dot_nt(q_ref[...], k) * HEAD_DIM ** -0.5
    z = jnp.concatenate([z8[:, t * LANES:(t + 1) * LANES] for t in range(nt)], axis=0)
    ls = jnp.minimum(z, 0.0) - jnp.log1p(jnp.exp(-jnp.abs(z)))
    both = _exact_dot(ls - z, tri_ref[...])
    tot = both[:, LANES:]
    later_tiles = _exact_dot(tot, t2_ref[...], left=False)
    later = both[:, :LANES] + later_tiles + jnp.concatenate([carry_sc[...]] * nt, axis=0)
    a = jnp.exp(ls + later)
    a8 = jnp.concatenate([a[t * QROWS:(t + 1) * QROWS] for t in range(nt)], axis=1)
    acc_sc[...] += _dot(a8, v)
    carry_sc[...] += later_tiles[0:QROWS] + tot[0:QROWS]

    @pl.when(c == pl.num_programs(1) - 1)
    def _():
        o_ref[...] = acc_sc[...]


def sb_sample(cache, layer, page_table, sb_q):
    s_n, n_pages = page_table.shape
    npp = min(PAGES_PER_STEP, n_pages)
    nc = n_pages // npp
    nt = npp * PAGE_ROWS // LANES
    rows = nt * QROWS
    q8 = jnp.pad(sb_q, ((0, 0), (0, QROWS - H_SB), (0, 0)))
    idx = lax.broadcasted_iota(jnp.int32, (LANES, LANES), 0) > lax.broadcasted_iota(jnp.int32, (LANES, LANES), 1)
    tri = jnp.concatenate([idx.astype(MXU_DTYPE), jnp.ones((LANES, LANES), MXU_DTYPE)], axis=1)
    r = lax.broadcasted_iota(jnp.int32, (rows, rows), 0)
    cidx = lax.broadcasted_iota(jnp.int32, (rows, rows), 1)
    t2 = ((r % QROWS == cidx % QROWS) & (cidx // QROWS > r // QROWS)).astype(MXU_DTYPE)
    grid_spec = pltpu.PrefetchScalarGridSpec(
        num_scalar_prefetch=1, grid=(s_n, nc),
        in_specs=[pl.BlockSpec((None, QROWS, HEAD_DIM), lambda s, c, pt: (s, 0, 0)),
                  pl.BlockSpec((LANES, 2 * LANES), lambda s, c, pt: (0, 0)),
                  pl.BlockSpec((rows, rows), lambda s, c, pt: (0, 0))]
                 + _page_specs(SB_KV_W, layer, npp, lambda c: nc - 1 - c),
        out_specs=pl.BlockSpec((None, QROWS, HEAD_DIM), lambda s, c, pt: (s, 0, 0)),
        scratch_shapes=[pltpu.VMEM((QROWS, LANES), F32), pltpu.VMEM((QROWS, HEAD_DIM), F32)])
    out = pl.pallas_call(
        functools.partial(_sb_s_kernel, npp=npp), grid_spec=grid_spec,
        out_shape=jax.ShapeDtypeStruct((s_n, QROWS, HEAD_DIM), F32),
        compiler_params=_cparams("parallel", "arbitrary"), name="sb_sample",
    )(page_table, _mx(q8), tri, t2, *([cache] * npp))
    return out[:, :H_SB].reshape(s_n, H_SB * HEAD_DIM)


def _first_indices_below(eq, idx, need, nbits):
    def count_lt(t):
        return sum(jnp.sum(jnp.where(e & (i < t), 1.0, 0.0), keepdims=True) for e, i in zip(eq, idx))

    def step(s, t):
        cand = t + (jnp.int32(1) << (nbits - 1 - s))
        return jnp.where(count_lt(cand) <= need, cand, t)

    return lax.fori_loop(0, nbits, step, jnp.zeros((1, 1), jnp.int32))


def _dsa_s_kernel(pt_ref, q_ref, qi_ref, w_ref, bias_ref, new_ref, *rest, npp, n_keep):
    pages = rest[:npp]
    o_ref, key_sc, kv_sc = rest[npp:]
    c = pl.program_id(1)
    nc = pl.num_programs(1)
    hd = HEAD_DIM
    n = npp * PAGE_ROWS
    scale = hd ** -0.5

    def index_keys(kidx, width):
        rel = jnp.maximum(_dot_nt(qi_ref[...], kidx), 0.0)
        return jnp.sum(_lanes(w_ref[...], width // LANES) * rel, axis=0, keepdims=True)

    kidx = jnp.concatenate([_mx(pg[:, 2 * hd:]) for pg in pages], axis=0)
    key_sc[pl.ds(c, 1), :] = _order_key(index_keys(kidx, n))
    kv_sc[c] = jnp.concatenate([_mx(pg[:, :2 * hd]) for pg in pages], axis=0)

    @pl.when(c == nc - 1)
    def _():
        nck = key_sc.shape[0]
        lane_n = lax.broadcasted_iota(jnp.int32, (1, LANES), 1)
        new = _mx(new_ref[...])
        key_new = _order_key(jnp.where(lane_n == 0, index_keys(new[:, 2 * hd:], LANES), NEG_INF))
        keys = key_sc[...]
        idx = lax.broadcasted_iota(jnp.int32, (nck, n), 0) * n + lax.broadcasted_iota(jnp.int32, (nck, n), 1)
        idx_new = nck * n + lane_n

        def count_ge(t):
            return (jnp.sum(jnp.where(keys >= t, 1.0, 0.0), keepdims=True)
                    + jnp.sum(jnp.where(key_new >= t, 1.0, 0.0), keepdims=True))

        t = _kth_threshold(count_ge, float(n_keep), 1)
        need = float(n_keep) - count_ge(t + 1)
        nbits = int(nck * n + LANES).bit_length()
        lim = _first_indices_below([keys == t, key_new == t], [idx, idx_new], need, nbits)
        sel = ((keys > t) | ((keys == t) & (idx < lim))) & (keys > KEY_NEG_INF)
        sel_new = ((key_new > t) | ((key_new == t) & (idx_new < lim))) & (key_new > KEY_NEG_INF)

        q = q_ref[...]
        m = jnp.full((QROWS, 1), NEG_INF, F32)
        l = jnp.zeros((QROWS, 1), F32)
        acc = jnp.zeros((QROWS, hd), F32)
        for cc in range(nck):
            kv = kv_sc[cc]
            s = _dot_nt(q, kv[:, :hd]) * scale + bias_ref[cc]
            m, l, p, alpha = _decode_softmax_step(s, sel[cc:cc + 1, :], m, l)
            acc = alpha * acc + _dot(p, kv[:, hd:])
        s = _dot_nt(q, new[:, :hd]) * scale + bias_ref[nck][:, :LANES]
        m, l, p, alpha = _decode_softmax_step(s, sel_new, m, l)
        acc = alpha * acc + _dot(p, new[:, hd:2 * hd])
        o_ref[...] = acc / jnp.where(l > 0, l, 1.0)


def dsa_sample(cache, layer, page_table, dsa_q, q_idx, w_idx, new_rows, bias_keys):
    s_n, n_pages = page_table.shape
    npp = min(PAGES_PER_STEP, n_pages)
    nc = n_pages // npp
    n = npp * PAGE_ROWS
    total = n_pages * PAGE_ROWS + 1
    pad_h = lambda x: jnp.pad(x, ((0, 0), (0, QROWS - x.shape[1]), (0, 0)))
    w8 = jnp.broadcast_to(pad_h(w_idx[:, :, None]), (s_n, QROWS, LANES))
    new_pad = jnp.pad(new_rows[:, None, :], ((0, 0), (0, LANES - 1), (0, 0)))
    grid_spec = pltpu.PrefetchScalarGridSpec(
        num_scalar_prefetch=1, grid=(s_n, nc),
        in_specs=[pl.BlockSpec((None, QROWS, HEAD_DIM), lambda s, c, pt: (s, 0, 0)),
                  pl.BlockSpec((None, QROWS, D_IDX), lambda s, c, pt: (s, 0, 0)),
                  pl.BlockSpec((None, QROWS, LANES), lambda s, c, pt: (s, 0, 0)),
                  pl.BlockSpec((nc + 1, QROWS, n), lambda s, c, pt: (0, 0, 0)),
                  pl.BlockSpec((None, LANES, DSA_KV_W), lambda s, c, pt: (s, 0, 0))]
                 + _page_specs(DSA_KV_W, layer, npp, lambda c: c),
        out_specs=pl.BlockSpec((None, QROWS, HEAD_DIM), lambda s, c, pt: (s, 0, 0)),
        scratch_shapes=[pltpu.VMEM((nc, n), jnp.int32), pltpu.VMEM((nc, n, 2 * HEAD_DIM), MXU_DTYPE)])
    out = pl.pallas_call(
        functools.partial(_dsa_s_kernel, npp=npp, n_keep=min(TOPK_KEYS, total // 4)), grid_spec=grid_spec,
        out_shape=jax.ShapeDtypeStruct((s_n, QROWS, HEAD_DIM), F32),
        compiler_params=_cparams("parallel", "arbitrary"), name="dsa_sample",
    )(page_table, _mx(pad_h(dsa_q)), _mx(pad_h(q_idx)), w8, bias_keys, new_pad, *([cache] * npp))
    return out[:, :H_DSA].reshape(s_n, H_DSA * HEAD_DIM)


def _pool_kernel(x_ref, pe_ref, w_ref, o_ref):
    x = x_ref[...]
    tok = CMP_STRIDE
    w = x.shape[1] // tok
    x2 = jnp.concatenate([x[:, t * w:t * w + 2 * HEAD_DIM] for t in range(tok)], axis=1)
    o_ref[:, :2 * CMP_HIDDEN] = _dot(x2 + pe_ref[0:1, :], w_ref[0])
    o_ref[:, 2 * CMP_HIDDEN:] = _dot(x2 + pe_ref[1:2, :], w_ref[1])


def pool_compress_hidden(cache, layer, w1, pe):
    depth, n_pool, page, width = cache.shape
    per_page = page // CMP_STRIDE
    xv = cache.reshape(depth, n_pool * per_page, CMP_STRIDE * width)
    w1r = w1.reshape(2, 2, CMP_STRIDE, HEAD_DIM, CMP_HIDDEN)
    wz = jnp.zeros((2, CMP_STRIDE, 2 * HEAD_DIM, 2 * CMP_HIDDEN), F32)
    wz = wz.at[:, :, :HEAD_DIM, :CMP_HIDDEN].set(w1r[0]).at[:, :, HEAD_DIM:, CMP_HIDDEN:].set(w1r[1])
    wz = _mx(wz.reshape(2, CMP_STRIDE * 2 * HEAD_DIM, 2 * CMP_HIDDEN))
    pez = jnp.transpose(pe.reshape(2, 2, CMP_STRIDE, HEAD_DIM), (1, 2, 0, 3)).reshape(2, CMP_STRIDE * 2 * HEAD_DIM)
    rows = n_pool * per_page
    tm = 512 if rows % 512 == 0 else rows
    return pl.pallas_call(
        _pool_kernel,
        grid=(rows // tm,),
        in_specs=[pl.BlockSpec((None, tm, CMP_STRIDE * width), lambda i: (layer, i, 0)),
                  pl.BlockSpec((2, CMP_STRIDE * 2 * HEAD_DIM), lambda i: (0, 0)),
                  pl.BlockSpec((2, CMP_STRIDE * 2 * HEAD_DIM, 2 * CMP_HIDDEN), lambda i: (0, 0, 0))],
        out_specs=pl.BlockSpec((tm, 4 * CMP_HIDDEN), lambda i: (i, 0)),
        out_shape=jax.ShapeDtypeStruct((rows, 4 * CMP_HIDDEN), F32),
        compiler_params=_cparams("parallel"), name="nsa_pool_hidden",
    )(xv, pez, wz).reshape(n_pool, per_page, 4 * CMP_HIDDEN)


def _nsa_s_kernel(pt_ref, q_ref, gate_ref, w2_ref, gk_ref, bc_ref, smat_ref, fold_ref, exp_ref, bs_ref, win_ref,
                  bw_ref, newn_ref, neww_ref, *rest, npp, past_len, n_sel):
    ab_pages = rest[:npp]
    ab_next = rest[npp]
    pages = rest[npp + 1:2 * npp + 1]
    o_ref, sc_sc, vc_sc, kv_sc, r_sc = rest[2 * npp + 1:]
    c = pl.program_id(1)
    nc = pl.num_programs(1)
    hd = HEAD_DIM
    scale = hd ** -0.5
    q = q_ref[...]
    ncb = npp * PAGE_ROWS // CMP_STRIDE

    ab = jnp.concatenate([a[...] for a in ab_pages], axis=0)
    second = ab[:, 2 * CMP_HIDDEN:]
    row = lax.broadcasted_iota(jnp.int32, (ncb, 1), 0)
    second = jnp.where(row == ncb - 1, ab_next[0:1, 2 * CMP_HIDDEN:], jnp.roll(second, -1, axis=0))
    hid = jax.nn.gelu(ab[:, :2 * CMP_HIDDEN] + second)
    kc = _dot(hid[:, :CMP_HIDDEN], w2_ref[0])
    kc = kc * lax.rsqrt(jnp.mean(kc * kc, axis=-1, keepdims=True) + EPS) * gk_ref[...]
    cmp_end = (c * ncb + lax.broadcasted_iota(jnp.int32, (1, ncb), 1)) * CMP_STRIDE + (CMP_BLOCK - 1)
    sc_sc[c] = jnp.where(cmp_end <= past_len, _dot_nt(q, kc) * scale + bc_ref[c], NEG_INF)
    vc_sc[c] = _dot(hid[:, CMP_HIDDEN:], w2_ref[1])
    kv_sc[c] = jnp.concatenate([_mx(pg[:, 2 * hd:]) for pg in pages], axis=0)

    @pl.when(c == nc - 1)
    def _():
        nck = sc_sc.shape[0]
        n = npp * PAGE_ROWS
        s = jnp.concatenate([sc_sc[cc] for cc in range(nck)], axis=1)
        m = jnp.max(s, axis=1, keepdims=True)
        e = jnp.exp(s - jnp.where(m == NEG_INF, 0.0, m))
        d = jnp.sum(e, axis=1, keepdims=True)
        pc = e / jnp.where(d > 0, d, 1.0)
        o_c = jnp.zeros((QROWS, hd), F32)
        for cc in range(nck):
            o_c = o_c + _dot(pc[:, cc * ncb:(cc + 1) * ncb], vc_sc[cc])
        imp = ((pc[0:1] + pc[1:2]) + pc[2:3]) + pc[3:4]
        score = _exact_dot(jnp.broadcast_to(imp, pc.shape), smat_ref[...])
        nsp = score.shape[1]
        blk = lax.broadcasted_iota(jnp.int32, (QROWS, nsp), 1)
        back = past_len // SLC_BLOCK - blk
        forced = (blk == 0) | ((back >= 0) & (back < N_LOCAL))
        score = jnp.where(back >= 0, jnp.where(forced, jnp.inf, score), NEG_INF)
        sel_f = jnp.where(_topk_mask_row(score, n_sel), 1.0, 0.0)
        bpc = n // SLC_BLOCK
        nrow = r_sc.shape[0]
        chunk_of_blk = lax.broadcasted_iota(jnp.int32, (nrow, nsp), 1) // bpc
        own = chunk_of_blk == lax.broadcasted_iota(jnp.int32, (nrow, nsp), 0)
        sel_rows = jnp.concatenate([sel_f] * (nrow // QROWS), axis=0)
        r_sc[...] = _dot(jnp.where(own, sel_rows, 0.0), fold_ref[...])

        st = (jnp.full((QROWS, 1), NEG_INF, F32), jnp.zeros((QROWS, 1), F32))
        acc = jnp.zeros((QROWS, hd), F32)
        for cc in range(nck):
            picked = _dot(jnp.broadcast_to(r_sc[cc:cc + 1, :], (QROWS, LANES)), exp_ref[...]) > 0.5
            kv = kv_sc[cc]
            sc = _dot_nt(q, kv[:, :hd]) * scale + bs_ref[cc]
            m2, l2, p, alpha = _decode_softmax_step(sc, picked, *st)
            st = (m2, l2)
            acc = alpha * acc + _dot(p, kv[:, hd:])
        lane_n = lax.broadcasted_iota(jnp.int32, (QROWS, LANES), 1)
        newn = _mx(newn_ref[...])
        nb_row, nb_lane = (past_len // SLC_BLOCK) // bpc, (past_len // SLC_BLOCK) % bpc
        own_blk = jnp.sum(jnp.where(lane_n[0:1] == nb_lane, r_sc[nb_row:nb_row + 1, :], 0.0), axis=1, keepdims=True)
        sc = _dot_nt(q, newn[:, 2 * hd:3 * hd]) * scale + bs_ref[nck][:, :LANES]
        m2, l2, p, alpha = _decode_softmax_step(sc, (lane_n == 0) & (own_blk > 0.5), *st)
        acc = alpha * acc + _dot(p, newn[:, 3 * hd:])
        o_s = acc / jnp.where(l2 > 0, l2, 1.0)

        win = _mx(win_ref[...])
        wl = win.shape[0]
        dist = wl - lax.broadcasted_iota(jnp.int32, (QROWS, wl), 1)
        sc = _dot_nt(q, win[:, :hd]) * scale + bw_ref[:, :wl]
        st = (jnp.full((QROWS, 1), NEG_INF, F32), jnp.zeros((QROWS, 1), F32))
        m3, l3, p, alpha = _decode_softmax_step(sc, dist < WINDOW, *st)
        acc = _dot(p, win[:, hd:])
        neww = _mx(neww_ref[...])
        sc = _dot_nt(q, neww[:, :hd]) * scale + bw_ref[:, wl:wl + LANES]
        m3, l3, p, alpha = _decode_softmax_step(sc, lane_n == 0, m3, l3)
        acc = alpha * acc + _dot(p, neww[:, hd:])
        o_w = acc / jnp.where(l3 > 0, l3, 1.0)

        g = gate_ref[...]
        o_ref[...] = g[:, 0:1] * o_c + g[:, 1:2] * o_s + g[:, 2:3] * o_w


def nsa_sample(cache, layer, page_table, hidden_pool, nsa_q, nsa_g, new_nsa, new_win, win_state, w2, g_k0,
               bias_cmp, bias_keys, bias_win, smat):
    s_n, n_pages = page_table.shape
    npp = min(PAGES_PER_STEP, n_pages)
    nc = n_pages // npp
    n = npp * PAGE_ROWS
    past_len = n_pages * PAGE_ROWS
    ncb = n // CMP_STRIDE
    per_page = PAGE_ROWS // CMP_STRIDE
    bpc = n // SLC_BLOCK
    nsp = smat.shape[1]
    nrow = -(-(nsp // bpc + 1) // QROWS) * QROWS
    pad_h = lambda x: jnp.pad(x, ((0, 0), (0, QROWS - x.shape[1]), (0, 0)))
    gate8 = jnp.pad(pad_h(nsa_g), ((0, 0), (0, 0), (0, LANES - 3)))
    pad_row = lambda x: jnp.pad(x[:, None, :], ((0, 0), (0, LANES - 1), (0, 0)))
    fold = (lax.broadcasted_iota(jnp.int32, (nsp, LANES), 0) % bpc
            == lax.broadcasted_iota(jnp.int32, (nsp, LANES), 1)).astype(MXU_DTYPE)
    expand = (lax.broadcasted_iota(jnp.int32, (LANES, n), 0)
              == lax.broadcasted_iota(jnp.int32, (LANES, n), 1) // SLC_BLOCK).astype(MXU_DTYPE)
    wl = win_state.shape[1]
    const = lambda shape: pl.BlockSpec(shape, lambda s, c, pt: (0,) * len(shape))
    ab_spec = lambda n_: pl.BlockSpec((None, per_page, 4 * CMP_HIDDEN), lambda s, c, pt: (pt[s, c * npp + n_], 0, 0))
    grid_spec = pltpu.PrefetchScalarGridSpec(
        num_scalar_prefetch=1, grid=(s_n, nc),
        in_specs=[pl.BlockSpec((None, QROWS, HEAD_DIM), lambda s, c, pt: (s, 0, 0)),
                  pl.BlockSpec((None, QROWS, LANES), lambda s, c, pt: (s, 0, 0)),
                  const((2, CMP_HIDDEN, HEAD_DIM)), const((1, HEAD_DIM)),
                  const((nc, QROWS, ncb)), const(smat.shape), const((nsp, LANES)), const((LANES, n)),
                  const((nc + 1, QROWS, n)),
                  pl.BlockSpec((None, wl, WIN_KV_W), lambda s, c, pt: (s, 0, 0)),
                  const((QROWS, wl + LANES)),
                  pl.BlockSpec((None, LANES, NSA_KV_W), lambda s, c, pt: (s, 0, 0)),
                  pl.BlockSpec((None, LANES, WIN_KV_W), lambda s, c, pt: (s, 0, 0))]
                 + [ab_spec(n_) for n_ in range(npp)]
                 + [pl.BlockSpec((None, per_page, 4 * CMP_HIDDEN),
                                 lambda s, c, pt: (pt[s, jnp.minimum((c + 1) * npp, n_pages - 1)], 0, 0))]
                 + _page_specs(NSA_KV_W, layer, npp, lambda c: c),
        out_specs=pl.BlockSpec((None, QROWS, HEAD_DIM), lambda s, c, pt: (s, 0, 0)),
        scratch_shapes=[pltpu.VMEM((nc, QROWS, ncb), F32), pltpu.VMEM((nc, ncb, HEAD_DIM), F32),
                        pltpu.VMEM((nc, n, 2 * HEAD_DIM), MXU_DTYPE), pltpu.VMEM((nrow, LANES), F32)])
    out = pl.pallas_call(
        functools.partial(_nsa_s_kernel, npp=npp, past_len=past_len,
                          n_sel=min(N_SELECT, -(-(past_len + 1) // SLC_BLOCK))),
        grid_spec=grid_spec,
        out_shape=jax.ShapeDtypeStruct((s_n, QROWS, HEAD_DIM), F32),
        compiler_params=_cparams("parallel", "arbitrary"), name="nsa_sample",
    )(page_table, _mx(pad_h(nsa_q)), gate8, _mx(w2), g_k0.reshape(1, HEAD_DIM), bias_cmp, smat, fold, expand,
      bias_keys, win_state, bias_win, pad_row(new_nsa), pad_row(new_win),
      *([hidden_pool] * (npp + 1)), *([cache] * npp))
    return out[:, :H_NSA].reshape(s_n, H_NSA * HEAD_DIM)


def rmsnorm(x, g):
    xf = x.astype(jnp.float32)
    y = xf * lax.rsqrt(jnp.mean(xf * xf, axis=-1, keepdims=True) + EPS)
    return y.astype(x.dtype) * g


def rope(x, pos):
    half = x.shape[-1] // 2
    freqs = ROPE_THETA ** (-jnp.arange(half, dtype=jnp.float32) / half)
    ang = pos.astype(jnp.float32)[..., None] * freqs
    cos, sin = jnp.cos(ang), jnp.sin(ang)
    xf = x.astype(jnp.float32)
    x1, x2 = xf[..., :half], xf[..., half:]
    return jnp.concatenate([x1 * cos - x2 * sin, x2 * cos + x1 * sin], axis=-1).astype(x.dtype)


def t5_bias(table, dist):
    n = jnp.maximum(dist, 0)
    exact = NUM_BUCKETS // 2
    big = exact + (jnp.log(jnp.maximum(n, exact).astype(jnp.float32) / exact)
                   / math.log(T5_MAX_DIST / exact) * (NUM_BUCKETS - exact)).astype(jnp.int32)
    bucket = jnp.where(n < exact, n, jnp.minimum(big, NUM_BUCKETS - 1))
    onehot = (bucket[..., None] == jnp.arange(NUM_BUCKETS, dtype=jnp.int32)).astype(jnp.float32)
    looked_up = jnp.einsum('...b,bh->...h', onehot, table.astype(jnp.float32), precision=lax.Precision.HIGHEST)
    return jnp.moveaxis(looked_up, -1, 0)


def masked_softmax(s, mask):
    s = jnp.where(mask, s, -jnp.inf)
    m = jnp.max(s, axis=-1, keepdims=True)
    e = jnp.exp(s - jnp.where(jnp.isfinite(m), m, 0.0))
    d = jnp.sum(e, axis=-1, keepdims=True)
    return e / jnp.where(d > 0, d, 1.0)


def sweep(fn, P, *qarrs):
    lq = qarrs[0].shape[0]
    qb = QBLOCK if lq % QBLOCK == 0 else lq
    qpos = P + jnp.arange(lq, dtype=jnp.int32)

    def body(b):
        start = b * qb
        blocks = [lax.dynamic_slice_in_dim(a, start, qb, 0) for a in qarrs]
        return fn(start, lax.dynamic_slice_in_dim(qpos, start, qb, 0), *blocks)

    out = lax.map(body, jnp.arange(lq // qb, dtype=jnp.int32))
    return out.reshape((lq,) + out.shape[2:])


def group_transforms(y, pos, p):
    B, L, _ = y.shape
    hd = HEAD_DIM
    seg = lambda c, w: y[..., c:c + w]
    mqa, mkv = seg(C_MQA, Q_LORA), seg(C_MKV, MLA_KV_W)
    nkv, dkv = seg(C_NKV, NSA_KV_W + WIN_KV_W), seg(C_DKV, 2 * hd)
    q = lin(mqa.reshape(B * L, Q_LORA), p['w_mla_uq'], g=p['g_mla_qa'], pro="rms",
            tm=min(512, B * L)).reshape(B, L, H_MLA, DN_MLA + DR_MLA)
    gq = p['g_mla_q']
    mla_q = jnp.concatenate([rmsnorm(q[..., :DN_MLA], gq[:DN_MLA]),
                             rope(rmsnorm(q[..., DN_MLA:], gq[DN_MLA:]), pos[:, None])], axis=-1)
    mla_rows = jnp.concatenate([rmsnorm(mkv[..., :KV_LORA], p['g_mla_kva']),
                                rope(rmsnorm(mkv[..., KV_LORA:], p['g_mla_k'][DN_MLA:]), pos)], axis=-1)
    sb_q = seg(C_SQ, H_SB * hd).reshape(B, L, H_SB, hd)
    nsa_q = rmsnorm(seg(C_NQ, H_NSA * hd).reshape(B, L, H_NSA, hd), p['g_nsa_q'])
    gk = p['g_nsa_k']
    nsa_rows = jnp.concatenate([nkv[..., :2 * hd], rmsnorm(nkv[..., 2 * hd:3 * hd], gk[1]), nkv[..., 3 * hd:4 * hd]], axis=-1)
    win_rows = jnp.concatenate([rmsnorm(nkv[..., 4 * hd:5 * hd], gk[2]), nkv[..., 5 * hd:]], axis=-1)
    nsa_g = jax.nn.sigmoid(seg(C_NG, 3 * H_NSA).reshape(B, L, H_NSA, 3))
    dsa_q = rmsnorm(seg(C_DQ, H_DSA * hd).reshape(B, L, H_DSA, hd), p['g_dsa_q'])
    dsa_rows = jnp.concatenate([rmsnorm(dkv[..., :hd], p['g_dsa_k']), dkv[..., hd:], seg(C_DIK, D_IDX)], axis=-1)
    q_idx = seg(C_DIQ, H_IDX * D_IDX).reshape(B, L, H_IDX, D_IDX)
    qs = (mla_q, sb_q, nsa_q, nsa_g, dsa_q, q_idx, seg(C_DIW, H_IDX))
    rows = (mla_rows, seg(C_SKV, SB_KV_W), nsa_rows, win_rows, dsa_rows)
    return qs, rows


def mla_attend(q, rows, P, w_ukv, g_kn):
    lk = rows.shape[0]
    kv = (rows[:, :KV_LORA] @ w_ukv).reshape(lk, H_MLA, DN_MLA + DV_MLA)
    k_n = rmsnorm(kv[..., :DN_MLA], g_kn)
    v = kv[..., DN_MLA:]
    k_r = rows[:, KV_LORA:]
    kpos = jnp.arange(lk, dtype=jnp.int32)
    scale = (DN_MLA + DR_MLA) ** -0.5

    def blk(start, qpos, qb):
        s = (jnp.einsum('qhd,khd->hqk', qb[..., :DN_MLA], k_n).astype(jnp.float32)
             + jnp.einsum('qhd,kd->hqk', qb[..., DN_MLA:], k_r).astype(jnp.float32)) * scale
        pr = masked_softmax(s, kpos[None, :] <= qpos[:, None])
        return jnp.einsum('hqk,khd->qhd', pr.astype(v.dtype), v).reshape(qb.shape[0], H_MLA * DV_MLA)

    return sweep(blk, P, q)


def sb_attend(q, rows, P):
    lk = rows.shape[0]
    k, v = rows[:, :HEAD_DIM], rows[:, HEAD_DIM:]
    kpos = jnp.arange(lk, dtype=jnp.int32)

    def blk(start, qpos, qb):
        z = jnp.einsum('qhd,kd->hqk', qb, k).astype(jnp.float32) * HEAD_DIM ** -0.5
        before = kpos[None, :] < qpos[:, None]
        log_beta = jnp.where(before, jax.nn.log_sigmoid(z), -jnp.inf)
        log_keep = jnp.where(before, jax.nn.log_sigmoid(-z), 0.0)
        later = lax.cumsum(log_keep, axis=2, reverse=True) - log_keep
        a = jnp.exp(log_beta + later)
        return jnp.einsum('hqk,kd->qhd', a.astype(v.dtype), v).reshape(qb.shape[0], H_SB * HEAD_DIM)

    return sweep(blk, P, q)


def nsa_attend(q, gate, nsa_rows, win_rows, P, w_len, p, table):
    hd = HEAD_DIM
    lk = nsa_rows.shape[0]
    scale = hd ** -0.5
    n_chunk = lk // CMP_STRIDE
    n_cmp = n_chunk - 1

    def compress_(x, i):
        ch = x[: n_chunk * CMP_STRIDE].reshape(n_chunk, CMP_STRIDE, hd)
        blocks = jnp.concatenate([ch[:-1], ch[1:]], axis=1) + p['pe_cmp'][i]
        return jax.nn.gelu(blocks.reshape(n_cmp, CMP_BLOCK * hd) @ p['w_cmp1'][i]) @ p['w_cmp2'][i]

    k_cmp = rmsnorm(compress_(nsa_rows[:, :hd], 0), p['g_nsa_k'][0])
    v_cmp = compress_(nsa_rows[:, hd:2 * hd], 1)
    cmp_end = jnp.arange(n_cmp, dtype=jnp.int32) * CMP_STRIDE + (CMP_BLOCK - 1)
    n_slc = -(-lk // SLC_BLOCK)
    n_sel = min(N_SELECT, n_slc)
    slc = jnp.pad(nsa_rows[:, 2 * hd:], ((0, n_slc * SLC_BLOCK - lk), (0, 0))).reshape(n_slc, SLC_BLOCK, 2 * hd)
    ratio = SLC_BLOCK // CMP_STRIDE
    span = CMP_BLOCK // CMP_STRIDE
    blk_ids = jnp.arange(n_slc, dtype=jnp.int32)
    lw = win_rows.shape[0]
    wpos = P - WINDOW + jnp.arange(lw, dtype=jnp.int32)
    wvalid = wpos >= P - w_len

    def blk(start, qpos, qb, gb):
        nq = qb.shape[0]
        qf = qpos[:, None]
        dist_c = qf - cmp_end[None, :]
        s = jnp.einsum('qhd,kd->hqk', qb, k_cmp).astype(jnp.float32) * scale + t5_bias(table, dist_c)
        pc = masked_softmax(s, dist_c >= 0)
        o_c = jnp.einsum('hqk,kd->qhd', pc.astype(v_cmp.dtype), v_cmp)
        imp = jnp.pad(pc.sum(0), ((0, 0), (span - 1, ratio * (n_slc + 1) - (span - 1) - n_cmp)))
        score = jnp.zeros((nq, n_slc), jnp.float32)
        for m in range(ratio):
            for n in range(span):
                score = score + imp[:, m + n: m + n + ratio * n_slc: ratio]
        back = qf // SLC_BLOCK - blk_ids[None, :]
        forced = (blk_ids[None, :] == 0) | ((back >= 0) & (back < N_LOCAL))
        score = jnp.where(back >= 0, jnp.where(forced, jnp.inf, score), -jnp.inf)
        top_s, top_i = lax.top_k(score, n_sel)
        g = slc[top_i].reshape(nq, n_sel * SLC_BLOCK, 2 * hd)
        kpos = (top_i[..., None] * SLC_BLOCK + jnp.arange(SLC_BLOCK, dtype=jnp.int32)).reshape(nq, -1)
        ok = jnp.repeat(top_s > -jnp.inf, SLC_BLOCK, axis=1) & (kpos <= qf)
        s = jnp.einsum('qhd,qkd->hqk', qb, g[..., :hd]).astype(jnp.float32) * scale + t5_bias(table, qf - kpos)
        ps = masked_softmax(s, ok[None])
        o_s = jnp.einsum('hqk,qkd->qhd', ps.astype(g.dtype), g[..., hd:])
        wk = lax.dynamic_slice_in_dim(win_rows, start, nq + WINDOW, 0)
        wp = lax.dynamic_slice_in_dim(wpos, start, nq + WINDOW, 0)
        wv = lax.dynamic_slice_in_dim(wvalid, start, nq + WINDOW, 0)
        dist_w = qf - wp[None, :]
        s = jnp.einsum('qhd,kd->hqk', qb, wk[:, :hd]).astype(jnp.float32) * scale + t5_bias(table, dist_w)
        pw = masked_softmax(s, (dist_w >= 0) & (dist_w < WINDOW) & wv[None, :])
        o_w = jnp.einsum('hqk,kd->qhd', pw.astype(wk.dtype), wk[:, hd:])
        o = gb[..., 0:1] * o_c + gb[..., 1:2] * o_s + gb[..., 2:3] * o_w
        return o.reshape(nq, H_NSA * hd)

    return sweep(blk, P, q, gate)


def dsa_attend(q, q_idx, w_idx, rows, P, table):
    hd = HEAD_DIM
    lk = rows.shape[0]
    k, v, k_idx = rows[:, :hd], rows[:, hd:2 * hd], rows[:, 2 * hd:]
    n_keep = min(TOPK_KEYS, lk // 4)
    kpos = jnp.arange(lk, dtype=jnp.int32)
    scale = hd ** -0.5

    def blk(start, qpos, qb, qib, wb):
        qf = qpos[:, None]
        rel = jax.nn.relu(jnp.einsum('qhd,kd->qhk', qib, k_idx).astype(jnp.float32))
        isc = jnp.einsum('qh,qhk->qk', wb.astype(jnp.float32), rel)
        isc = jnp.where(kpos[None, :] <= qf, isc, -jnp.inf)
        top_s, top_i = lax.top_k(isc, n_keep)
        s = jnp.einsum('qhd,qkd->hqk', qb, k[top_i]).astype(jnp.float32) * scale + t5_bias(table, qf - top_i)
        pr = masked_softmax(s, (top_s > -jnp.inf)[None])
        return jnp.einsum('hqk,qkd->qhd', pr.astype(v.dtype), v[top_i]).reshape(qb.shape[0], H_DSA * hd)

    return sweep(blk, P, q, q_idx, w_idx)


def attend_seq(qs, rows, past, P, w_len, p, table):
    mla_q, sb_q, nsa_q, nsa_g, dsa_q, q_idx, w_idx = qs
    mla_r, sb_r, nsa_r, win_r, dsa_r = [jnp.concatenate([a, b], axis=0) for a, b in zip(past, rows)]
    win_r = jnp.pad(win_r, ((WINDOW - w_len, 0), (0, 0)))
    o_mla = mla_attend(mla_q, mla_r, P, p['w_mla_ukv'], p['g_mla_k'][:DN_MLA])
    o_sb = sb_attend(sb_q, sb_r, P)
    o_nsa = nsa_attend(nsa_q, nsa_g, nsa_r, win_r, P, w_len, p, table[:, :H_NSA])
    o_dsa = dsa_attend(dsa_q, q_idx, w_idx, dsa_r, P, table[:, H_NSA:])
    return (o_mla, o_sb, o_nsa, o_dsa)


def bias_tiles(table):
    r = jnp.arange(TILE, dtype=jnp.int32)
    tiles = [t5_bias(table, d * TILE + r[:, None] - r[None, :]) for d in (0, 1)]
    far = t5_bias(table, jnp.full((TILE, TILE), 2 * TILE + T5_MAX_DIST, jnp.int32))
    return jnp.stack(tiles + [far], axis=0)


def selection_matrix(ncp, n_cmp, n_slc, nsp):
    ratio = SLC_BLOCK // CMP_STRIDE
    span = CMP_BLOCK // CMP_STRIDE
    s = np.zeros((ncp, nsp), np.float32)
    for b in range(n_slc):
        for m in range(ratio):
            for n in range(span):
                c = ratio * b + m + n - (span - 1)
                if 0 <= c < n_cmp:
                    s[c, b] += 1.0
    return jnp.asarray(s, MXU_DTYPE)


def prompt_mixers(qs, rows, p, tb_nsa, tb_dsa, bias_c, smat):
    mla_q, sb_q, nsa_q, nsa_g, dsa_q, q_idx, w_idx = qs
    mla_r, sb_r, nsa_r, win_r, dsa_r = rows
    B, L = mla_r.shape[:2]
    hd = HEAD_DIM
    heads_first = lambda x: jnp.transpose(x, (0, 2, 1, 3))
    kv = lin(mla_r[..., :KV_LORA].reshape(B * L, KV_LORA), p['w_mla_ukv'], tm=min(512, B * L))
    kv = kv.reshape(B, L, H_MLA, DN_MLA + DV_MLA)
    k_n = rmsnorm(kv[..., :DN_MLA], p['g_mla_k'][:DN_MLA])
    k_r = jnp.broadcast_to(mla_r[:, :, None, KV_LORA:], (B, L, H_MLA, DR_MLA))
    zpad = jnp.zeros((B, L, H_MLA, LANES - DN_MLA - DR_MLA), F32)
    k_pack = _mx(jnp.concatenate([k_n, k_r, zpad], axis=-1)).reshape(B, L, H_MLA * LANES)
    q_pack = _mx(jnp.concatenate([mla_q, zpad], axis=-1)).reshape(B, L, H_MLA * LANES)
    v_pack = _mx(kv[..., DN_MLA:]).reshape(B, L, H_MLA * DV_MLA)
    o_mla = mla_prompt(q_pack, k_pack, v_pack)
    o_sb = sb_prompt(heads_first(sb_q), sb_r[..., :hd], sb_r[..., hd:])
    n_chunk = L // CMP_STRIDE
    n_cmp = n_chunk - 1
    ncp = smat.shape[0]
    raw = jnp.stack([nsa_r[..., :hd], nsa_r[..., hd:2 * hd]], axis=1).reshape(B, 2, n_chunk, CMP_STRIDE * hd)
    blocks = jnp.concatenate([raw[:, :, :-1], raw[:, :, 1:]], axis=-1) + p['pe_cmp'].reshape(1, 2, 1, CMP_BLOCK * hd)
    blocks = jnp.pad(blocks, ((0, 0), (0, 0), (0, ncp - n_cmp), (0, 0)))
    cmp_kv = compress(blocks, p['w_cmp1'], p['w_cmp2'], p['g_nsa_k'][0])
    o_nsa = nsa_prompt(heads_first(nsa_q), heads_first(nsa_g), cmp_kv, bias_c,
                       nsa_r[..., 2 * hd:3 * hd], nsa_r[..., 3 * hd:], win_r[..., :hd], win_r[..., hd:], smat, tb_nsa)
    o_dsa = dsa_prompt(heads_first(dsa_q), heads_first(q_idx), w_idx, dsa_r[..., :hd], dsa_r[..., hd:2 * hd],
                       dsa_r[..., 2 * hd:], tb_dsa)
    return tuple(o.reshape(B * L, BR_WIDTH) for o in (o_mla, o_sb, o_nsa, o_dsa))


def pack_w_in(w):
    cuts = np.cumsum((0,) + IN_SIZES)
    cols = []
    for i in range(10):
        cols.append(jnp.pad(w[:, cuts[i]:cuts[i + 1]], ((0, 0), (0, _SEG_PAD[i] - IN_SIZES[i]))))
    tail = w[:, cuts[10]:cuts[12]]
    cols.append(jnp.pad(tail, ((0, 0), (0, N_SMALL - C_DIK - tail.shape[1]))))
    cols.append(w[:, cuts[12]:])
    return _mx(jnp.concatenate(cols, axis=1))


def kernel(x_prompt, x_sample, cache_mla, cache_sb, cache_nsa, cache_dsa, state_nsa_win, page_table,
           c_prompt, c_sample, w_ada, b_ada, g_norm_mix, g_norm_ffn, w_in, g_mla_qa, w_mla_uq, g_mla_kva,
           w_mla_ukv, g_mla_q, g_mla_k, w_cmp1, w_cmp2, pe_cmp, g_nsa_q, g_nsa_k, g_dsa_q, g_dsa_k, rel_bias,
           w_br, w_out, w_router, b_router, w_gu, b_gu, w_down, b_down):
    B, L, D = x_prompt.shape
    S, seq_s, _ = x_sample.shape
    past_len = page_table.shape[1] * cache_mla.shape[2]
    w_len = state_nsa_win.shape[2]
    w_len_p = min(WINDOW, L)
    pos_p = jnp.arange(L, dtype=jnp.int32)
    pos_s = past_len + jnp.arange(seq_s, dtype=jnp.int32)
    tm_p = min(256, L)
    tpg_p = L // tm_p
    ts = S * seq_s

    tb_nsa = bias_tiles(rel_bias[:, :H_NSA])
    tb_dsa = bias_tiles(rel_bias[:, H_NSA:])
    n_cmp = L // CMP_STRIDE - 1
    ncp = -(-n_cmp // LANES) * LANES
    n_slc = -(-L // SLC_BLOCK)
    nsp = -(-n_slc // LANES) * LANES
    cmp_end = jnp.arange(ncp, dtype=jnp.int32) * CMP_STRIDE + (CMP_BLOCK - 1)
    bias_c = t5_bias(rel_bias[:, :H_NSA], pos_p[:, None] - cmp_end[None, :])
    smat = selection_matrix(ncp, n_cmp, n_slc, nsp)

    assert seq_s == 1
    n_pages = page_table.shape[1]
    npp = min(PAGES_PER_STEP, n_pages)
    nc_s = n_pages // npp
    n_s = npp * PAGE_ROWS
    pad_heads = lambda b: jnp.pad(b, [(0, 0)] * (b.ndim - 2) + [(0, QROWS - b.shape[-2]), (0, 0)])
    kpos = jnp.arange((nc_s + 1) * n_s, dtype=jnp.int32).reshape(nc_s + 1, n_s)
    by_key = lambda tab: pad_heads(jnp.transpose(t5_bias(tab, past_len - kpos), (1, 0, 2)))
    sb_nsa = by_key(rel_bias[:, :H_NSA])
    sb_dsa = by_key(rel_bias[:, H_NSA:])
    ncb_s = n_s // CMP_STRIDE
    cmp_end_s = jnp.arange(nc_s * ncb_s, dtype=jnp.int32).reshape(nc_s, ncb_s) * CMP_STRIDE + (CMP_BLOCK - 1)
    sb_cmp = pad_heads(jnp.transpose(t5_bias(rel_bias[:, :H_NSA], past_len - cmp_end_s), (1, 0, 2)))
    wdist = jnp.concatenate([w_len - jnp.arange(w_len, dtype=jnp.int32), -jnp.arange(LANES, dtype=jnp.int32)])
    sb_win = pad_heads(t5_bias(rel_bias[:, :H_NSA], wdist))
    n_slc_s = -(-(past_len + 1) // SLC_BLOCK)
    smat_s = selection_matrix(nc_s * ncb_s, (past_len + 1) // CMP_STRIDE - 1, n_slc_s, -(-n_slc_s // LANES) * LANES)

    src_col = jnp.arange(2 * D_FF, dtype=jnp.int32)
    src_col = jnp.where(src_col < D_FF, 2 * src_col, 2 * (src_col - D_FF) + 1)
    deinterleave = (jnp.arange(2 * D_FF, dtype=jnp.int32)[:, None] == src_col[None, :]).astype(MXU_DTYPE)

    xp = x_prompt.reshape(B * L, D)
    xs = x_sample.reshape(ts, D)
    c_all = jnp.concatenate([c_prompt, c_sample], axis=0)
    out_p = []
    out_s = []
    for l in range(DEPTH):
        p = {'g_mla_qa': g_mla_qa[l], 'w_mla_uq': w_mla_uq[l], 'g_mla_kva': g_mla_kva[l],
             'w_mla_ukv': w_mla_ukv[l], 'g_mla_q': g_mla_q[l], 'g_mla_k': g_mla_k[l], 'w_cmp1': w_cmp1[l],
             'w_cmp2': w_cmp2[l], 'pe_cmp': pe_cmp[l], 'g_nsa_q': g_nsa_q[l], 'g_nsa_k': g_nsa_k[l],
             'g_dsa_q': g_dsa_q[l], 'g_dsa_k': g_dsa_k[l]}
        mod = lin(c_all, w_ada[l], b=b_ada[l], pro="silu", tn=6 * D // 4)
        mod_p = [m.reshape(B, 1, D) for m in jnp.split(mod[:B], 6, axis=-1)]
        mod_s = [m.reshape(1, ts, D) for m in jnp.split(mod[B:], 6, axis=-1)]
        w_p = pack_w_in(w_in[l])

        y_p, gates_p = proj_in(xp, g_norm_mix[l], mod_p[0], mod_p[1], w_p, tm_p, tpg_p)
        qs_p, rows_p = group_transforms(y_p.reshape(B, L, N_SMALL), pos_p, p)
        o4_p = prompt_mixers(qs_p, rows_p, p, tb_nsa, tb_dsa, bias_c, smat)
        xp = merge(o4_p, gates_p, w_br[l], w_out[l], xp, mod_p[2], tm_p, tpg_p)

        y_s, gates_s = proj_in(xs, g_norm_mix[l], mod_s[0], mod_s[1], w_p, ts, 1)
        qs_s, rows_s = group_transforms(y_s.reshape(S, seq_s, N_SMALL), pos_s, p)

        sq = [a[:, 0] for a in qs_s]
        sr = [a[:, 0] for a in rows_s]
        hidden_pool = pool_compress_hidden(cache_nsa, l, w_cmp1[l], pe_cmp[l])
        o4_s = (mla_sample(cache_mla, l, page_table, sq[0], sr[0], w_mla_ukv[l], g_mla_k[l][:DN_MLA]),
                sb_sample(cache_sb, l, page_table, sq[1]),
                nsa_sample(cache_nsa, l, page_table, hidden_pool, sq[2], sq[3], sr[2], sr[3], state_nsa_win[l],
                           w_cmp2[l], g_nsa_k[l][0], sb_cmp, sb_nsa, sb_win, smat_s),
                dsa_sample(cache_dsa, l, page_table, sq[4], sq[5], sq[6], sr[4], sb_dsa))
        xs = merge(o4_s, gates_s, w_br[l], w_out[l], xs, mod_s[2], ts, 1)

        h_p, lg_p = router(xp, g_norm_ffn[l], mod_p[3], mod_p[4], w_router[l], b_router[l], tm_p, tpg_p)
        h_s, lg_s = router(xs, g_norm_ffn[l], mod_s[3], mod_s[4], w_router[l], b_router[l], ts, 1)
        wg = lin(w_gu[l].reshape(N_EXPERTS * D, 2 * D_FF), deinterleave, tm=1024,
                 out_dtype=MXU_DTYPE).reshape(N_EXPERTS, D, 2 * D_FF)
        bg = jnp.concatenate([b_gu[l][:, 0::2], b_gu[l][:, 1::2]], axis=-1)[:, None, :]
        y_all = moe_ffn(jnp.concatenate([h_p, h_s], axis=0),
                        jnp.concatenate([lg_p, lg_s], axis=0)[:, :N_EXPERTS],
                        wg, bg, _mx(w_down[l]), b_down[l][:, None, :])
        xp = xp + jnp.broadcast_to(mod_p[5], (B, L, D)).reshape(B * L, D) * y_all[:B * L]
        xs = xs + mod_s[5].reshape(ts, D) * y_all[B * L:]

        out_p.append((rows_p[0], rows_p[1], rows_p[2], rows_p[4], rows_p[3][:, L - w_len_p:]))
        win_s = jnp.concatenate([state_nsa_win[l], rows_s[3]], axis=1)
        out_s.append((rows_s[0], rows_s[1], rows_s[2], rows_s[4], win_s[:, win_s.shape[1] - w_len:]))
    new_p = [jnp.stack([o[i] for o in out_p], axis=0) for i in range(5)]
    new_s = [jnp.stack([o[i] for o in out_s], axis=0) for i in range(5)]
    return (xp.reshape(B, L, D), xs.reshape(S, seq_s, D), *new_p, *new_s)
```

```python
import functools
import math
import jax, jax.numpy as jnp
from jax import lax
import numpy as np
from jax.experimental import pallas as pl
from jax.experimental.pallas import tpu as pltpu

D_MODEL = 1024
DEPTH = 2
HEAD_DIM = 64
H_MLA = 4
Q_LORA = 256
KV_LORA = 128
DN_MLA = 64
DR_MLA = 32
DV_MLA = 64
ROPE_THETA = 10000.0
H_SB = 4
H_NSA = 4
CMP_STRIDE = 16
CMP_BLOCK = 2 * CMP_STRIDE
CMP_HIDDEN = 128
SLC_BLOCK = 64
N_SELECT = 16
N_LOCAL = 2
WINDOW = 512
H_DSA = 4
H_IDX = 4
D_IDX = 32
TOPK_KEYS = 256
NUM_BUCKETS = 32
T5_MAX_DIST = 128
N_EXPERTS = 32
TOP_K = 4
D_FF = 512
SWIGLU_LIMIT = 7.0
SWIGLU_ALPHA = 1.702
MOE_BLOCK = 128
QBLOCK = 128
N_BRANCH = 4
BR_WIDTH = 256
EPS = 1e-6
MLA_KV_W = KV_LORA + DR_MLA
SB_KV_W = 2 * HEAD_DIM
NSA_KV_W = 4 * HEAD_DIM
WIN_KV_W = 2 * HEAD_DIM
DSA_KV_W = 2 * HEAD_DIM + D_IDX
IN_SIZES = (Q_LORA, MLA_KV_W, H_SB * HEAD_DIM, SB_KV_W, H_NSA * HEAD_DIM, NSA_KV_W + WIN_KV_W, 3 * H_NSA,
            H_DSA * HEAD_DIM, 2 * HEAD_DIM, H_IDX * D_IDX, D_IDX, H_IDX, N_BRANCH * D_MODEL)

MXU_DTYPE = jnp.bfloat16
F32 = jnp.float32
NEG_INF = float("-inf")
LANES = 128
TILE = LANES
VMEM_LIMIT_BYTES = 56 * 1024 * 1024
MOE_ROWS = 256
INT_MIN = -2 ** 31
SB_DEAD_LOG = -110.0

_SEG_PAD = (256, 256, 256, 128, 256, 384, 128, 256, 128, 128, 0, 0)
C_MQA, C_MKV, C_SQ, C_SKV, C_NQ, C_NKV, C_NG, C_DQ, C_DKV, C_DIQ, C_DIK = (
    0, 256, 512, 768, 896, 1152, 1536, 1664, 1920, 2048, 2176)
C_DIW = C_DIK + D_IDX
N_SMALL = 2304


def _mx(a):
    return a.astype(MXU_DTYPE)


def _dot(a, b):
    return jnp.dot(_mx(a), _mx(b), preferred_element_type=F32)


def _dot_nt(a, b):
    return lax.dot_general(_mx(a), _mx(b), (((1,), (1,)), ((), ())), preferred_element_type=F32)


def _exact_dot(a, b, left=True):
    b = b.astype(MXU_DTYPE)
    f = (lambda u: jnp.dot(u, b, preferred_element_type=F32)) if left else (
        lambda u: jnp.dot(b, u, preferred_element_type=F32))
    if MXU_DTYPE == F32:
        return f(a)
    hi = a.astype(MXU_DTYPE)
    r1 = a - hi.astype(F32)
    mid = r1.astype(MXU_DTYPE)
    lo = (r1 - mid.astype(F32)).astype(MXU_DTYPE)
    return f(hi) + f(mid) + f(lo)


def _cparams(*sem):
    return pltpu.CompilerParams(dimension_semantics=sem, vmem_limit_bytes=VMEM_LIMIT_BYTES)


def _adaln_rows(x, g, sh, sc):
    y = x * lax.rsqrt(jnp.mean(x * x, axis=-1, keepdims=True) + EPS)
    return y * g * (1.0 + sc) + sh


def _proj_in_kernel(x_ref, g_ref, sh_ref, sc_ref, w_ref, ys_ref, gate_ref):
    h = _mx(_adaln_rows(x_ref[...], g_ref[...], sh_ref[...], sc_ref[...]))
    ys_ref[...] = jnp.dot(h, w_ref[:, :N_SMALL], preferred_element_type=F32)
    for c in range(N_BRANCH):
        lo = N_SMALL + c * D_MODEL
        gate_ref[:, c * D_MODEL:(c + 1) * D_MODEL] = jax.nn.sigmoid(
            jnp.dot(h, w_ref[:, lo:lo + D_MODEL], preferred_element_type=F32))


def _mod_specs(mod, tm, tiles_per_group):
    r = mod.shape[1]
    return pl.BlockSpec((None, r, mod.shape[2]), lambda i: (i // tiles_per_group, 0, 0))


def proj_in(x2d, g, sh, sc, w_p, tm, tiles_per_group):
    t, d = x2d.shape
    n = w_p.shape[1]
    return pl.pallas_call(
        _proj_in_kernel,
        grid=(t // tm,),
        in_specs=[pl.BlockSpec((tm, d), lambda i: (i, 0)),
                  pl.BlockSpec((1, d), lambda i: (0, 0)),
                  _mod_specs(sh, tm, tiles_per_group), _mod_specs(sc, tm, tiles_per_group),
                  pl.BlockSpec((d, n), lambda i: (0, 0))],
        out_specs=[pl.BlockSpec((tm, N_SMALL), lambda i: (i, 0)),
                   pl.BlockSpec((tm, N_BRANCH * D_MODEL), lambda i: (i, 0))],
        out_shape=[jax.ShapeDtypeStruct((t, N_SMALL), F32), jax.ShapeDtypeStruct((t, N_BRANCH * D_MODEL), F32)],
        compiler_params=_cparams("parallel"),
        name="proj_in",
    )(x2d, g.reshape(1, d), sh, sc, w_p)


def _lin_kernel(*refs, pro, has_g, has_b):
    refs = list(refs)
    x_ref = refs.pop(0)
    g_ref = refs.pop(0) if has_g else None
    w_ref = refs.pop(0)
    b_ref = refs.pop(0) if has_b else None
    o_ref = refs.pop(0)
    x = x_ref[...]
    if pro == "silu":
        x = x * jax.nn.sigmoid(x)
    elif pro == "rms":
        x = x * lax.rsqrt(jnp.mean(x * x, axis=-1, keepdims=True) + EPS) * g_ref[...]
    y = _dot(x, w_ref[...])
    if has_b:
        y = y + b_ref[...]
    o_ref[...] = y.astype(o_ref.dtype)


def lin(x, w, g=None, b=None, pro="none", tm=512, tn=None, out_dtype=F32):
    m, k = x.shape
    n = w.shape[1]
    tm = min(tm, m)
    tn = n if tn is None else tn
    args = [x]
    specs = [pl.BlockSpec((tm, k), lambda i, j: (i, 0))]
    if g is not None:
        args.append(g.reshape(1, k))
        specs.append(pl.BlockSpec((1, k), lambda i, j: (0, 0)))
    args.append(_mx(w))
    specs.append(pl.BlockSpec((k, tn), lambda i, j: (0, j)))
    if b is not None:
        args.append(b.reshape(1, n))
        specs.append(pl.BlockSpec((1, tn), lambda i, j: (0, j)))
    return pl.pallas_call(
        functools.partial(_lin_kernel, pro=pro, has_g=g is not None, has_b=b is not None),
        grid=(m // tm, n // tn),
        in_specs=specs,
        out_specs=pl.BlockSpec((tm, tn), lambda i, j: (i, j)),
        out_shape=jax.ShapeDtypeStruct((m, n), out_dtype),
        compiler_params=_cparams("parallel", "parallel"),
        name="lin_" + pro,
    )(*args)


def _router_kernel(x_ref, g_ref, sh_ref, sc_ref, w_ref, b_ref, h_ref, lg_ref):
    h = _adaln_rows(x_ref[...], g_ref[...], sh_ref[...], sc_ref[...])
    h_ref[...] = h.astype(h_ref.dtype)
    lg_ref[...] = _dot(h, w_ref[...]) + b_ref[...]


def router(x2d, g, sh, sc, w_r, b_r, tm, tiles_per_group):
    t, d = x2d.shape
    wp = jnp.pad(_mx(w_r), ((0, 0), (0, LANES - N_EXPERTS)))
    bp = jnp.pad(b_r.reshape(1, -1), ((0, 0), (0, LANES - N_EXPERTS)), constant_values=NEG_INF)
    return pl.pallas_call(
        _router_kernel,
        grid=(t // tm,),
        in_specs=[pl.BlockSpec((tm, d), lambda i: (i, 0)),
                  pl.BlockSpec((1, d), lambda i: (0, 0)),
                  _mod_specs(sh, tm, tiles_per_group), _mod_specs(sc, tm, tiles_per_group),
                  pl.BlockSpec((d, LANES), lambda i: (0, 0)),
                  pl.BlockSpec((1, LANES), lambda i: (0, 0))],
        out_specs=[pl.BlockSpec((tm, d), lambda i: (i, 0)), pl.BlockSpec((tm, LANES), lambda i: (i, 0))],
        out_shape=[jax.ShapeDtypeStruct((t, d), MXU_DTYPE), jax.ShapeDtypeStruct((t, LANES), F32)],
        compiler_params=_cparams("parallel"),
        name="router",
    )(x2d, g.reshape(1, d), sh, sc, wp, bp)


def _moe_kernel(be_ref, x_ref, wg_ref, bg_ref, wd_ref, bd_ref, o_ref):
    gu = _dot(x_ref[...], wg_ref[...]) + bg_ref[...]
    glu = jnp.minimum(gu[:, :D_FF], SWIGLU_LIMIT)
    lin_ = jnp.clip(gu[:, D_FF:], -SWIGLU_LIMIT, SWIGLU_LIMIT)
    act = glu * jax.nn.sigmoid(SWIGLU_ALPHA * glu) * (lin_ + 1.0)
    o_ref[...] = _dot(act, wd_ref[...]) + bd_ref[...]


def moe_blocks(xg, blk_e, wg, bg, wd, bd):
    n, d = xg.shape
    nb = n // MOE_ROWS
    grid_spec = pltpu.PrefetchScalarGridSpec(
        num_scalar_prefetch=1,
        grid=(nb,),
        in_specs=[pl.BlockSpec((MOE_ROWS, d), lambda b, be: (b, 0)),
                  pl.BlockSpec((None, d, 2 * D_FF), lambda b, be: (be[b], 0, 0)),
                  pl.BlockSpec((None, 1, 2 * D_FF), lambda b, be: (be[b], 0, 0)),
                  pl.BlockSpec((None, D_FF, d), lambda b, be: (be[b], 0, 0)),
                  pl.BlockSpec((None, 1, d), lambda b, be: (be[b], 0, 0))],
        out_specs=pl.BlockSpec((MOE_ROWS, d), lambda b, be: (b, 0)))
    return pl.pallas_call(
        _moe_kernel, grid_spec=grid_spec,
        out_shape=jax.ShapeDtypeStruct((n, d), F32),
        compiler_params=_cparams("arbitrary"),
        name="moe_ffn",
    )(blk_e, xg, wg, bg, wd, bd)


def moe_ffn(h, logits, wg, bg, wd, bd):
    t, d = h.shape
    top_v, top_e = lax.top_k(logits, TOP_K)
    wts = jax.nn.softmax(top_v, axis=-1)
    a = t * TOP_K
    flat_e = top_e.reshape(-1)
    onehot = (flat_e[:, None] == jnp.arange(N_EXPERTS, dtype=jnp.int32)[None, :]).astype(jnp.int32)
    rank = jnp.sum((jnp.cumsum(onehot, axis=0) - onehot) * onehot, axis=1)
    counts = jnp.sum(onehot, axis=0)
    padded = (counts + MOE_ROWS - 1) // MOE_ROWS * MOE_ROWS
    pad_end = jnp.cumsum(padded)
    pad_start = pad_end - padded
    dest = (pad_start[flat_e] + rank).astype(jnp.int32)
    nb = -(-a // MOE_ROWS) + N_EXPERTS
    slot_tok = jnp.full((nb * MOE_ROWS,), t, jnp.int32).at[dest].set(jnp.arange(a, dtype=jnp.int32) // TOP_K)
    blk_e = jnp.minimum(jnp.searchsorted(pad_end, jnp.arange(nb, dtype=jnp.int32) * MOE_ROWS, side='right'),
                        N_EXPERTS - 1).astype(jnp.int32)
    hpad = jnp.concatenate([h, jnp.zeros((1, d), h.dtype)], axis=0)
    out = moe_blocks(hpad[slot_tok], blk_e, wg, bg, wd, bd)
    dest = dest.reshape(t, TOP_K)
    y = wts[:, 0:1] * out[dest[:, 0]]
    for k in range(1, TOP_K):
        y = y + wts[:, k:k + 1] * out[dest[:, k]]
    return y


def _merge_kernel(o0_ref, o1_ref, o2_ref, o3_ref, gates_ref, wbr_ref, wout_ref, x_ref, gm_ref, out_ref):
    acc = None
    for b, o_ref in enumerate((o0_ref, o1_ref, o2_ref, o3_ref)):
        t = gates_ref[:, b * D_MODEL:(b + 1) * D_MODEL] * _dot(o_ref[...], wbr_ref[b])
        acc = t if acc is None else acc + t
    out_ref[...] = x_ref[...] + gm_ref[...] * _dot(acc, wout_ref[...])


def merge(o4, gates, w_br, w_out, x2d, gm, tm, tiles_per_group):
    t, d = x2d.shape
    ospec = pl.BlockSpec((tm, BR_WIDTH), lambda i: (i, 0))
    return pl.pallas_call(
        _merge_kernel,
        grid=(t // tm,),
        in_specs=[ospec, ospec, ospec, ospec,
                  pl.BlockSpec((tm, N_BRANCH * d), lambda i: (i, 0)),
                  pl.BlockSpec((N_BRANCH, BR_WIDTH, d), lambda i: (0, 0, 0)),
                  pl.BlockSpec((d, d), lambda i: (0, 0)),
                  pl.BlockSpec((tm, d), lambda i: (i, 0)),
                  _mod_specs(gm, tm, tiles_per_group)],
        out_specs=pl.BlockSpec((tm, d), lambda i: (i, 0)),
        out_shape=jax.ShapeDtypeStruct((t, d), F32),
        compiler_params=_cparams("parallel"),
        name="merge",
    )(*o4, gates, _mx(w_br), _mx(w_out), x2d, gm)


def _order_key(x):
    bits = lax.bitcast_convert_type(x, jnp.int32)
    return jnp.where(bits >= 0, bits, bits ^ jnp.int32(0x7FFFFFFF))


KEY_NEG_INF = int(np.array(-np.inf, np.float32).view(np.int32)) ^ 0x7FFFFFFF
KEY_NEG_INF = KEY_NEG_INF - 2 ** 32 if KEY_NEG_INF >= 2 ** 31 else KEY_NEG_INF


def _kth_threshold(count_ge, k, rows):
    zero = jnp.zeros((rows, 1), jnp.int32)
    t = jnp.where(count_ge(zero) >= k, zero, jnp.full((rows, 1), INT_MIN, jnp.int32))

    def step(s, t):
        cand = t + (jnp.int32(1) << (30 - s))
        return jnp.where(count_ge(cand) >= k, cand, t)

    return lax.fori_loop(0, 31, step, t)


def _lower_tri(n):
    return (lax.broadcasted_iota(jnp.int32, (n, n), 0) < lax.broadcasted_iota(jnp.int32, (n, n), 1)).astype(MXU_DTYPE)


def _topk_mask_row(score, k):
    r, c = score.shape
    key = _order_key(score)

    def count_ge(t):
        return jnp.sum(jnp.where(key >= t, 1.0, 0.0), axis=1, keepdims=True)

    t = _kth_threshold(count_ge, float(k), r)
    gt = key > t
    eq = key == t
    need = float(k) - jnp.sum(jnp.where(gt, 1.0, 0.0), axis=1, keepdims=True)
    prefix = jnp.dot(jnp.where(eq, 1.0, 0.0).astype(MXU_DTYPE), _lower_tri(c), preferred_element_type=F32)
    return (gt | (eq & (prefix < need))) & (key > KEY_NEG_INF)


def _softmax_step(s, ok, m, l, acc, v):
    s = jnp.where(ok, s, NEG_INF)
    m_new = jnp.maximum(m, jnp.max(s, axis=1, keepdims=True))
    m_safe = jnp.where(m_new == NEG_INF, 0.0, m_new)
    p = jnp.exp(s - m_safe)
    alpha = jnp.exp(m - m_safe)
    return m_new, alpha * l + jnp.sum(p, axis=1, keepdims=True), alpha * acc + _dot(p, v)


def _tile_bias(tb_ref, d, heads):
    b = jnp.where(d == 0, tb_ref[0], jnp.where(d == 1, tb_ref[1], tb_ref[2]))
    return b.reshape(heads * TILE, TILE)


def _store_heads(o_ref, o, heads, rows, width):
    for h in range(heads):
        o_ref[:, h * width:(h + 1) * width] = o[h * rows:(h + 1) * rows]


def _mla_kernel(q_ref, k_ref, v_ref, o_ref, m_sc, l_sc, acc_sc, *, tq, tk, scale):
    i = pl.program_id(1)
    j = pl.program_id(2)

    @pl.when(j == 0)
    def _():
        m_sc[...] = jnp.full(m_sc.shape, NEG_INF, F32)
        l_sc[...] = jnp.zeros(l_sc.shape, F32)
        acc_sc[...] = jnp.zeros(acc_sc.shape, F32)

    @pl.when(j * tk < (i + 1) * tq)
    def _():
        qpos = i * tq + lax.broadcasted_iota(jnp.int32, (tq, tk), 0)
        kpos = j * tk + lax.broadcasted_iota(jnp.int32, (tq, tk), 1)
        causal = kpos <= qpos
        for h in range(H_MLA):
            s = _dot_nt(q_ref[:, h * LANES:(h + 1) * LANES], k_ref[:, h * LANES:(h + 1) * LANES]) * scale
            m, l, acc = _softmax_step(s, causal, m_sc[h], l_sc[h], acc_sc[h], v_ref[:, h * DV_MLA:(h + 1) * DV_MLA])
            m_sc[h] = m
            l_sc[h] = l
            acc_sc[h] = acc

    @pl.when(j == pl.num_programs(2) - 1)
    def _():
        for h in range(H_MLA):
            o_ref[:, h * DV_MLA:(h + 1) * DV_MLA] = acc_sc[h] / l_sc[h]


def mla_prompt(q, k, v):
    b, l, _ = q.shape
    tq = min(256, l)
    tk = min(512, l)
    last = lambda i: ((i + 1) * tq - 1) // tk
    return pl.pallas_call(
        functools.partial(_mla_kernel, tq=tq, tk=tk, scale=(DN_MLA + DR_MLA) ** -0.5),
        grid=(b, l // tq, l // tk),
        in_specs=[pl.BlockSpec((None, tq, H_MLA * LANES), lambda bb, i, j: (bb, i, 0)),
                  pl.BlockSpec((None, tk, H_MLA * LANES), lambda bb, i, j: (bb, jnp.minimum(j, last(i)), 0)),
                  pl.BlockSpec((None, tk, H_MLA * DV_MLA), lambda bb, i, j: (bb, jnp.minimum(j, last(i)), 0))],
        out_specs=pl.BlockSpec((None, tq, H_MLA * DV_MLA), lambda bb, i, j: (bb, i, 0)),
        out_shape=jax.ShapeDtypeStruct((b, l, H_MLA * DV_MLA), F32),
        scratch_shapes=[pltpu.VMEM((H_MLA, tq, 1), F32), pltpu.VMEM((H_MLA, tq, 1), F32),
                        pltpu.VMEM((H_MLA, tq, DV_MLA), F32)],
        compiler_params=_cparams("parallel", "parallel", "arbitrary"),
        name="mla_prompt",
    )(q, k, v)


def _sb_kernel(q_ref, k_ref, v_ref, tri_ref, o_ref, carry_sc, acc_sc, *, tq, nsub):
    i = pl.program_id(1)
    s_id = pl.program_id(2)
    ks = nsub * TILE
    kb = ((i + 1) * tq - 1) // ks - s_id
    rows = H_SB * tq

    @pl.when(s_id == 0)
    def _():
        carry_sc[...] = jnp.zeros(carry_sc.shape, F32)
        acc_sc[...] = jnp.zeros(acc_sc.shape, F32)

    live = jnp.max(carry_sc[...]) > SB_DEAD_LOG

    @pl.when((kb >= 0) & ((s_id == 0) | live))
    def _():
        q = q_ref[...].reshape(rows, HEAD_DIM)
        qpos = i * tq + (lax.broadcasted_iota(jnp.int32, (rows, TILE), 0) & (tq - 1))
        lane = lax.broadcasted_iota(jnp.int32, (rows, TILE), 1)
        for u in reversed(range(nsub)):
            kstart = kb * ks + u * TILE

            @pl.when(kstart < (i + 1) * tq)
            def _():
                z = _dot_nt(q, k_ref[u]) * HEAD_DIM ** -0.5
                before = (kstart + lane) < qpos
                ls = jnp.minimum(z, 0.0) - jnp.log1p(jnp.exp(-jnp.abs(z)))
                lk = jnp.where(before, ls - z, 0.0)
                both = _exact_dot(lk, tri_ref[...])
                later = both[:, :TILE] + carry_sc[...]
                a = jnp.where(before, jnp.exp(ls + later), 0.0)
                acc_sc[...] += _dot(a, v_ref[u])
                carry_sc[...] += both[:, TILE:]

    @pl.when(s_id == pl.num_programs(2) - 1)
    def _():
        _store_heads(o_ref, acc_sc[...], H_SB, tq, HEAD_DIM)


def sb_prompt(q4, k, v):
    b, _, l, hd = q4.shape
    tq = TILE
    nsub = min(4, l // TILE)
    ks = nsub * TILE
    nt = l // ks
    idx = lax.broadcasted_iota(jnp.int32, (TILE, TILE), 0) > lax.broadcasted_iota(jnp.int32, (TILE, TILE), 1)
    tri = jnp.concatenate([idx.astype(MXU_DTYPE), jnp.ones((TILE, TILE), MXU_DTYPE)], axis=1)
    kmap = lambda bb, i, s: (bb, jnp.maximum(((i + 1) * tq - 1) // ks - s, 0), 0, 0)
    k4 = _mx(k).reshape(b, nt * nsub, TILE, hd)
    v4 = _mx(v).reshape(b, nt * nsub, TILE, hd)
    return pl.pallas_call(
        functools.partial(_sb_kernel, tq=tq, nsub=nsub),
        grid=(b, l // tq, nt),
        in_specs=[pl.BlockSpec((None, H_SB, tq, hd), lambda bb, i, s: (bb, 0, i, 0)),
                  pl.BlockSpec((None, nsub, TILE, hd), kmap),
                  pl.BlockSpec((None, nsub, TILE, hd), kmap),
                  pl.BlockSpec((TILE, 2 * TILE), lambda bb, i, s: (0, 0))],
        out_specs=pl.BlockSpec((None, tq, H_SB * hd), lambda bb, i, s: (bb, i, 0)),
        out_shape=jax.ShapeDtypeStruct((b, l, H_SB * hd), F32),
        scratch_shapes=[pltpu.VMEM((H_SB * tq, TILE), F32), pltpu.VMEM((H_SB * tq, hd), F32)],
        compiler_params=_cparams("parallel", "parallel", "arbitrary"),
        name="sb_prompt",
    )(_mx(q4), k4, v4, tri)


def _cmp_kernel(x_ref, w1_ref, w2_ref, g_ref, o_ref):
    y = _dot(jax.nn.gelu(_dot(x_ref[...], w1_ref[...])), w2_ref[...])
    yn = y * lax.rsqrt(jnp.mean(y * y, axis=-1, keepdims=True) + EPS) * g_ref[...]
    o_ref[...] = jnp.where(pl.program_id(1) == 0, yn, y)


def compress(blocks, w1, w2, g):
    b, _, nc, kdim = blocks.shape
    return pl.pallas_call(
        _cmp_kernel,
        grid=(b, 2),
        in_specs=[pl.BlockSpec((None, None, nc, kdim), lambda bb, i: (bb, i, 0, 0)),
                  pl.BlockSpec((None, kdim, CMP_HIDDEN), lambda bb, i: (i, 0, 0)),
                  pl.BlockSpec((None, CMP_HIDDEN, HEAD_DIM), lambda bb, i: (i, 0, 0)),
                  pl.BlockSpec((1, HEAD_DIM), lambda bb, i: (0, 0))],
        out_specs=pl.BlockSpec((None, None, nc, HEAD_DIM), lambda bb, i: (bb, i, 0, 0)),
        out_shape=jax.ShapeDtypeStruct((b, 2, nc, HEAD_DIM), F32),
        compiler_params=_cparams("parallel", "parallel"),
        name="nsa_compress",
    )(blocks, _mx(w1), _mx(w2), g.reshape(1, HEAD_DIM))


def _nsa_kernel(q_ref, gate_ref, kc_ref, vc_ref, bc_ref, ks_ref, vs_ref, kw_ref, vw_ref, smat_ref, tb_ref, o_ref,
                *, n_sel):
    i = pl.program_id(1)
    rows = H_NSA * TILE
    scale = HEAD_DIM ** -0.5
    ncp = kc_ref.shape[0]
    nsp = smat_ref.shape[1]
    q = q_ref[...].reshape(rows, HEAD_DIM)
    qp4 = i * TILE + (lax.broadcasted_iota(jnp.int32, (rows, 1), 0) & (TILE - 1))
    qp = i * TILE + lax.broadcasted_iota(jnp.int32, (TILE, 1), 0)
    lane = lax.broadcasted_iota(jnp.int32, (TILE, TILE), 1)
    rep4 = lambda x: jnp.concatenate([x] * H_NSA, axis=0)

    cmp_end = lax.broadcasted_iota(jnp.int32, (1, ncp), 1) * CMP_STRIDE + (CMP_BLOCK - 1)
    s = _dot_nt(q, kc_ref[...]) * scale + bc_ref[...].reshape(rows, ncp)
    s = jnp.where(qp4 >= cmp_end, s, NEG_INF)
    m = jnp.max(s, axis=1, keepdims=True)
    e = jnp.exp(s - jnp.where(m == NEG_INF, 0.0, m))
    d = jnp.sum(e, axis=1, keepdims=True)
    pc = e / jnp.where(d > 0, d, 1.0)
    o_c = _dot(pc, vc_ref[...])

    imp = ((pc[0:TILE] + pc[TILE:2 * TILE]) + pc[2 * TILE:3 * TILE]) + pc[3 * TILE:4 * TILE]
    score = _exact_dot(imp, smat_ref[...])
    blk = lax.broadcasted_iota(jnp.int32, (TILE, nsp), 1)
    back = qp // SLC_BLOCK - blk
    forced = (blk == 0) | ((back >= 0) & (back < N_LOCAL))
    score = jnp.where(back >= 0, jnp.where(forced, jnp.inf, score), NEG_INF)
    sel_f = jnp.where(_topk_mask_row(score, n_sel), 1.0, 0.0)

    zero_state = (jnp.full((rows, 1), NEG_INF, F32), jnp.zeros((rows, 1), F32), jnp.zeros((rows, HEAD_DIM), F32))
    blk_row = lax.broadcasted_iota(jnp.int32, (nsp, TILE), 0)
    half = lax.broadcasted_iota(jnp.int32, (nsp, TILE), 1) // SLC_BLOCK

    def slc_body(j, st):
        expand = jnp.where(blk_row == j * (TILE // SLC_BLOCK) + half, 1.0, 0.0)
        picked = _dot(sel_f, expand) > 0.5
        ok = picked & ((j * TILE + lane) <= qp)
        sc = _dot_nt(q, ks_ref[j]) * scale + _tile_bias(tb_ref, i - j, H_NSA)
        return _softmax_step(sc, rep4(ok), *st, vs_ref[j])

    _, l_s, acc_s = lax.fori_loop(0, i + 1, slc_body, zero_state)
    o_s = acc_s / jnp.where(l_s > 0, l_s, 1.0)

    def win_body(j, st):
        dist = qp - (j * TILE + lane)
        ok = (dist >= 0) & (dist < WINDOW)
        sc = _dot_nt(q, kw_ref[j]) * scale + _tile_bias(tb_ref, i - j, H_NSA)
        return _softmax_step(sc, rep4(ok), *st, vw_ref[j])

    _, l_w, acc_w = lax.fori_loop(jnp.maximum(i - WINDOW // TILE, 0), i + 1, win_body, zero_state)
    o_w = acc_w / jnp.where(l_w > 0, l_w, 1.0)

    g = gate_ref[...].reshape(rows, 3)
    o = g[:, 0:1] * o_c + g[:, 1:2] * o_s + g[:, 2:3] * o_w
    _store_heads(o_ref, o, H_NSA, TILE, HEAD_DIM)


def nsa_prompt(q4, gate4, cmp_kv, bias_c, k_slc, v_slc, k_win, v_win, smat, tb):
    b, _, l, hd = q4.shape
    nt = l // TILE
    ncp = cmp_kv.shape[2]
    nsp = smat.shape[1]
    n_sel = min(N_SELECT, l // SLC_BLOCK)
    tiles = lambda x: _mx(x).reshape(b, nt, TILE, hd)
    kv_spec = pl.BlockSpec((None, nt, TILE, hd), lambda bb, i: (bb, 0, 0, 0))
    return pl.pallas_call(
        functools.partial(_nsa_kernel, n_sel=n_sel),
        grid=(b, nt),
        in_specs=[pl.BlockSpec((None, H_NSA, TILE, hd), lambda bb, i: (bb, 0, i, 0)),
                  pl.BlockSpec((None, H_NSA, TILE, 3), lambda bb, i: (bb, 0, i, 0)),
                  pl.BlockSpec((None, None, ncp, hd), lambda bb, i: (bb, 0, 0, 0)),
                  pl.BlockSpec((None, None, ncp, hd), lambda bb, i: (bb, 1, 0, 0)),
                  pl.BlockSpec((H_NSA, TILE, ncp), lambda bb, i: (0, i, 0)),
                  kv_spec, kv_spec, kv_spec, kv_spec,
                  pl.BlockSpec((ncp, nsp), lambda bb, i: (0, 0)),
                  pl.BlockSpec((3, H_NSA, TILE, TILE), lambda bb, i: (0, 0, 0, 0))],
        out_specs=pl.BlockSpec((None, TILE, H_NSA * hd), lambda bb, i: (bb, i, 0)),
        out_shape=jax.ShapeDtypeStruct((b, l, H_NSA * hd), F32),
        compiler_params=_cparams("parallel", "parallel"),
        name="nsa_prompt",
    )(_mx(q4), gate4, _mx(cmp_kv), _mx(cmp_kv), bias_c, tiles(k_slc), tiles(v_slc), tiles(k_win), tiles(v_win),
      smat, tb)


def _dsa_kernel(q_ref, qi_ref, w_ref, kd_ref, vd_ref, ki_ref, tb_ref, o_ref, key_sc, *, n_keep):
    i = pl.program_id(1)
    rows = H_DSA * TILE
    scale = HEAD_DIM ** -0.5
    q = q_ref[...].reshape(rows, HEAD_DIM)
    qp = i * TILE + lax.broadcasted_iota(jnp.int32, (TILE, 1), 0)
    lane = lax.broadcasted_iota(jnp.int32, (TILE, TILE), 1)
    w = w_ref[...]

    def index_body(j, c):
        ki = ki_ref[j]
        isc = None
        for h in range(H_IDX):
            t = w[:, h:h + 1] * jnp.maximum(_dot_nt(qi_ref[h], ki), 0.0)
            isc = t if isc is None else isc + t
        isc = jnp.where((j * TILE + lane) <= qp, isc, NEG_INF)
        key_sc[j] = _order_key(isc)
        return c

    lax.fori_loop(0, i + 1, index_body, 0)

    def count_ge(t):
        def body(j, c):
            return c + jnp.where(key_sc[j] >= t, 1.0, 0.0)
        return jnp.sum(lax.fori_loop(0, i + 1, body, jnp.zeros((TILE, TILE), F32)), axis=1, keepdims=True)

    t = _kth_threshold(count_ge, float(n_keep), TILE)
    need = float(n_keep) - count_ge(t + 1)
    tri = _lower_tri(TILE)
    rep4 = lambda x: jnp.concatenate([x] * H_DSA, axis=0)

    def att_body(j, st):
        m, l, acc, seen = st
        key = key_sc[j]
        eq = key == t
        eq_f = jnp.where(eq, 1.0, 0.0)
        prefix = _dot(eq_f, tri) + seen
        ok = ((key > t) | (eq & (prefix < need))) & (key > KEY_NEG_INF)
        sc = _dot_nt(q, kd_ref[j]) * scale + _tile_bias(tb_ref, i - j, H_DSA)
        m, l, acc = _softmax_step(sc, rep4(ok), m, l, acc, vd_ref[j])
        return m, l, acc, seen + jnp.sum(eq_f, axis=1, keepdims=True)

    st0 = (jnp.full((rows, 1), NEG_INF, F32), jnp.zeros((rows, 1), F32), jnp.zeros((rows, HEAD_DIM), F32),
           jnp.zeros((TILE, 1), F32))
    _, l, acc, _ = lax.fori_loop(0, i + 1, att_body, st0)
    _store_heads(o_ref, acc / jnp.where(l > 0, l, 1.0), H_DSA, TILE, HEAD_DIM)


def dsa_prompt(q4, qi4, w_idx, k, v, k_idx, tb):
    b, _, l, hd = q4.shape
    nt = l // TILE
    n_keep = min(TOPK_KEYS, l // 4)
    tiles = lambda x: _mx(x).reshape(b, nt, TILE, x.shape[-1])
    kv_spec = lambda wd: pl.BlockSpec((None, nt, TILE, wd), lambda bb, i: (bb, 0, 0, 0))
    return pl.pallas_call(
        functools.partial(_dsa_kernel, n_keep=n_keep),
        grid=(b, nt),
        in_specs=[pl.BlockSpec((None, H_DSA, TILE, hd), lambda bb, i: (bb, 0, i, 0)),
                  pl.BlockSpec((None, H_IDX, TILE, D_IDX), lambda bb, i: (bb, 0, i, 0)),
                  pl.BlockSpec((None, TILE, H_IDX), lambda bb, i: (bb, i, 0)),
                  kv_spec(hd), kv_spec(hd), kv_spec(D_IDX),
                  pl.BlockSpec((3, H_DSA, TILE, TILE), lambda bb, i: (0, 0, 0, 0))],
        out_specs=pl.BlockSpec((None, TILE, H_DSA * hd), lambda bb, i: (bb, i, 0)),
        out_shape=jax.ShapeDtypeStruct((b, l, H_DSA * hd), F32),
        scratch_shapes=[pltpu.VMEM((nt, TILE, TILE), jnp.int32)],
        compiler_params=_cparams("parallel", "arbitrary"),
        name="dsa_prompt",
    )(_mx(q4), _mx(qi4), w_idx, tiles(k), tiles(v), tiles(k_idx), tb)


PAGES_PER_STEP = 16
QROWS = 8


def _page_specs(width, layer, npp, chunk_of, transposed=False):
    shape = (None, None, width, PAGE_ROWS) if transposed else (None, None, PAGE_ROWS, width)

    def spec(n):
        return pl.BlockSpec(shape, lambda s, c, pt: (layer, pt[s, chunk_of(c) * npp + n], 0, 0))
    return [spec(n) for n in range(npp)]


PAGE_ROWS = 128


def _lanes(x, times):
    return x if times == 1 else jnp.concatenate([x] * times, axis=1)


def _decode_softmax_step(s, ok, m, l):
    s = jnp.where(ok, s, NEG_INF)
    m_new = jnp.maximum(m, jnp.max(s, axis=1, keepdims=True))
    m_safe = jnp.where(m_new == NEG_INF, 0.0, m_new)
    p = jnp.exp(s - m_safe)
    alpha = jnp.exp(m - m_safe)
    return m_new, alpha * l + jnp.sum(p, axis=1, keepdims=True), p, alpha


def _mla_s_kernel(pt_ref, qg_ref, qr_ref, wt_ref, new_ref, *rest, npp, sub):
    pages = rest[:npp]
    o_ref, m_sc, l_sc, acc_sc = rest[npp:]
    c = pl.program_id(1)
    scale = (DN_MLA + DR_MLA) ** -0.5
    hw = DN_MLA + DV_MLA

    @pl.when(c == 0)
    def _():
        m_sc[...] = jnp.full(m_sc.shape, NEG_INF, F32)
        l_sc[...] = jnp.zeros(l_sc.shape, F32)
        acc_sc[...] = jnp.zeros(acc_sc.shape, F32)

    w_ext = jnp.concatenate([wt_ref[...], _mx(qr_ref[...])], axis=0)

    def process(rows_t, ok):
        n = rows_t.shape[1]
        kvt = jnp.dot(w_ext, rows_t, preferred_element_type=F32)
        s_rows = []
        for h in range(H_MLA):
            kk = kvt[h * hw:h * hw + DN_MLA]
            ss = jnp.sum(kk * kk, axis=0, keepdims=True)
            a = jnp.sum(kk * _lanes(qg_ref[h * DN_MLA:(h + 1) * DN_MLA, :], n // LANES), axis=0, keepdims=True)
            s_rows.append(a * lax.rsqrt(ss / DN_MLA + EPS))
        s = jnp.concatenate(s_rows + [jnp.zeros((QROWS - H_MLA, n), F32)], axis=0)
        s = (s + kvt[H_MLA * hw:H_MLA * hw + QROWS]) * scale
        m, l, p, alpha = _decode_softmax_step(s, ok, m_sc[...], l_sc[...])
        m_sc[...] = m
        l_sc[...] = l
        for h in range(H_MLA):
            pv = kvt[h * hw + DN_MLA:(h + 1) * hw] * p[h:h + 1, :]
            part = pv[:, 0:LANES]
            for t in range(1, n // LANES):
                part = part + pv[:, t * LANES:(t + 1) * LANES]
            acc_sc[h] = alpha[h:h + 1, :] * acc_sc[h] + part

    for g in range(npp // sub):
        process(jnp.concatenate([_mx(pg[...]) for pg in pages[g * sub:(g + 1) * sub]], axis=1), True)

    @pl.when(c == pl.num_programs(1) - 1)
    def _():
        process(_mx(new_ref[...]), lax.broadcasted_iota(jnp.int32, (1, LANES), 1) == 0)
        l = l_sc[...]
        for h in range(H_MLA):
            o = jnp.sum(acc_sc[h], axis=1, keepdims=True) / l[h:h + 1, :]
            o_ref[h * DV_MLA:(h + 1) * DV_MLA, :] = jnp.broadcast_to(o, (DV_MLA, LANES))


def mla_sample(cache, layer, page_table, mla_q, new_rows, w_ukv, g_kn):
    s_n, n_pages = page_table.shape
    npp = min(PAGES_PER_STEP, n_pages)
    hw = DN_MLA + DV_MLA
    qg = (mla_q[..., :DN_MLA] * g_kn).reshape(s_n, H_MLA * DN_MLA, 1)
    qg = jnp.broadcast_to(qg, (s_n, H_MLA * DN_MLA, LANES))
    qr = jnp.pad(mla_q[..., DN_MLA:], ((0, 0), (0, QROWS - H_MLA), (KV_LORA, 0)))
    wt = _mx(jnp.pad(w_ukv.T, ((0, 0), (0, DR_MLA))))
    new_pad = jnp.pad(new_rows[:, :, None], ((0, 0), (0, 0), (0, LANES - 1)))
    grid_spec = pltpu.PrefetchScalarGridSpec(
        num_scalar_prefetch=1, grid=(s_n, n_pages // npp),
        in_specs=[pl.BlockSpec((None, H_MLA * DN_MLA, LANES), lambda s, c, pt: (s, 0, 0)),
                  pl.BlockSpec((None, QROWS, MLA_KV_W), lambda s, c, pt: (s, 0, 0)),
                  pl.BlockSpec((H_MLA * hw, MLA_KV_W), lambda s, c, pt: (0, 0)),
                  pl.BlockSpec((None, MLA_KV_W, LANES), lambda s, c, pt: (s, 0, 0))]
                 + _page_specs(MLA_KV_W, layer, npp, lambda c: c, transposed=True),
        out_specs=pl.BlockSpec((None, H_MLA * DV_MLA, LANES), lambda s, c, pt: (s, 0, 0)),
        scratch_shapes=[pltpu.VMEM((QROWS, 1), F32), pltpu.VMEM((QROWS, 1), F32),
                        pltpu.VMEM((H_MLA, DV_MLA, LANES), F32)])
    out = pl.pallas_call(
        functools.partial(_mla_s_kernel, npp=npp, sub=min(4, npp)), grid_spec=grid_spec,
        out_shape=jax.ShapeDtypeStruct((s_n, H_MLA * DV_MLA, LANES), F32),
        compiler_params=_cparams("parallel", "arbitrary"), name="mla_sample",
    )(page_table, qg, qr, wt, new_pad, *([cache] * npp))
    return out[:, :, 0]


def _sb_s_kernel(pt_ref, q_ref, tri_ref, t2_ref, *rest, npp):
    pages = rest[:npp]
    o_ref, carry_sc, acc_sc = rest[npp:]
    c = pl.program_id(1)
    nt = npp * PAGE_ROWS // LANES

    @pl.when(c == 0)
    def _():
        carry_sc[...] = jnp.zeros(carry_sc.shape, F32)
        acc_sc[...] = jnp.zeros(acc_sc.shape, F32)

    @pl.when(jnp.max(carry_sc[...]) > SB_DEAD_LOG)
    def _():
        k = jnp.concatenate([_mx(pg[:, :HEAD_DIM]) for pg in pages], axis=0)
        v = jnp.concatenate([_mx(pg[:, HEAD_DIM:]) for pg in pages], axis=0)
        z8 = _dot_nt(q_ref[...], k) * HEAD_DIM ** -0.5
        z = jnp.concatenate([z8[:, t * LANES:(t + 1) * LANES] for t in range(nt)], axis=0)
        ls = jnp.minimum(z, 0.0) - jnp.log1p(jnp.exp(-jnp.abs(z)))
        both = _exact_dot(ls - z, tri_ref[...])
        tot = both[:, LANES:]
        later_tiles = _exact_dot(tot, t2_ref[...], left=False)
        later = both[:, :LANES] + later_tiles + jnp.concatenate([carry_sc[...]] * nt, axis=0)
        a = jnp.exp(ls + later)
        a8 = jnp.concatenate([a[t * QROWS:(t + 1) * QROWS] for t in range(nt)], axis=1)
        acc_sc[...] += _dot(a8, v)
        carry_sc[...] += later_tiles[0:QROWS] + tot[0:QROWS]

    @pl.when(c == pl.num_programs(1) - 1)
    def _():
        o_ref[...] = acc_sc[...]


def sb_sample(cache, layer, page_table, sb_q):
    s_n, n_pages = page_table.shape
    npp = min(PAGES_PER_STEP, n_pages)
    nc = n_pages // npp
    nt = npp * PAGE_ROWS // LANES
    rows = nt * QROWS
    q8 = jnp.pad(sb_q, ((0, 0), (0, QROWS - H_SB), (0, 0)))
    idx = lax.broadcasted_iota(jnp.int32, (LANES, LANES), 0) > lax.broadcasted_iota(jnp.int32, (LANES, LANES), 1)
    tri = jnp.concatenate([idx.astype(MXU_DTYPE), jnp.ones((LANES, LANES), MXU_DTYPE)], axis=1)
    r = lax.broadcasted_iota(jnp.int32, (rows, rows), 0)
    cidx = lax.broadcasted_iota(jnp.int32, (rows, rows), 1)
    t2 = ((r % QROWS == cidx % QROWS) & (cidx // QROWS > r // QROWS)).astype(MXU_DTYPE)
    grid_spec = pltpu.PrefetchScalarGridSpec(
        num_scalar_prefetch=1, grid=(s_n, nc),
        in_specs=[pl.BlockSpec((None, QROWS, HEAD_DIM), lambda s, c, pt: (s, 0, 0)),
                  pl.BlockSpec((LANES, 2 * LANES), lambda s, c, pt: (0, 0)),
                  pl.BlockSpec((rows, rows), lambda s, c, pt: (0, 0))]
                 + _page_specs(SB_KV_W, layer, npp, lambda c: nc - 1 - c),
        out_specs=pl.BlockSpec((None, QROWS, HEAD_DIM), lambda s, c, pt: (s, 0, 0)),
        scratch_shapes=[pltpu.VMEM((QROWS, LANES), F32), pltpu.VMEM((QROWS, HEAD_DIM), F32)])
    out = pl.pallas_call(
        functools.partial(_sb_s_kernel, npp=npp), grid_spec=grid_spec,
        out_shape=jax.ShapeDtypeStruct((s_n, QROWS, HEAD_DIM), F32),
        compiler_params=_cparams("parallel", "arbitrary"), name="sb_sample",
    )(page_table, _mx(q8), tri, t2, *([cache] * npp))
    return out[:, :H_SB].reshape(s_n, H_SB * HEAD_DIM)


def _first_indices_below(eq, idx, need, nbits):
    def count_lt(t):
        return sum(jnp.sum(jnp.where(e & (i < t), 1.0, 0.0), keepdims=True) for e, i in zip(eq, idx))

    def step(s, t):
        cand = t + (jnp.int32(1) << (nbits - 1 - s))
        return jnp.where(count_lt(cand) <= need, cand, t)

    return lax.fori_loop(0, nbits, step, jnp.zeros((1, 1), jnp.int32))


def _dsa_s_kernel(pt_ref, q_ref, qi_ref, w_ref, bias_ref, new_ref, *rest, npp, n_keep):
    pages = rest[:npp]
    o_ref, key_sc, kv_sc = rest[npp:]
    c = pl.program_id(1)
    nc = pl.num_programs(1)
    hd = HEAD_DIM
    n = npp * PAGE_ROWS
    scale = hd ** -0.5

    def index_keys(kidx_t, width):
        rel = jnp.maximum(_dot(qi_ref[...], kidx_t), 0.0)
        return jnp.sum(_lanes(w_ref[...], width // LANES) * rel, axis=0, keepdims=True)

    kidx_t = jnp.concatenate([_mx(pg[2 * hd:, :]) for pg in pages], axis=1)
    key_sc[pl.ds(c, 1), :] = _order_key(index_keys(kidx_t, n))
    kv_sc[c] = jnp.concatenate([_mx(pg[:2 * hd, :]) for pg in pages], axis=1)

    @pl.when(c == nc - 1)
    def _():
        nck = key_sc.shape[0]
        lane_n = lax.broadcasted_iota(jnp.int32, (1, LANES), 1)
        new = _mx(new_ref[...])
        key_new = _order_key(jnp.where(lane_n == 0, index_keys(new[2 * hd:, :], LANES), NEG_INF))
        keys = key_sc[...]
        idx = lax.broadcasted_iota(jnp.int32, (nck, n), 0) * n + lax.broadcasted_iota(jnp.int32, (nck, n), 1)
        idx_new = nck * n + lane_n

        def count_ge(t):
            return (jnp.sum(jnp.where(keys >= t, 1.0, 0.0), keepdims=True)
                    + jnp.sum(jnp.where(key_new >= t, 1.0, 0.0), keepdims=True))

        t = _kth_threshold(count_ge, float(n_keep), 1)
        need = float(n_keep) - count_ge(t + 1)
        nbits = int(nck * n + LANES).bit_length()
        lim = _first_indices_below([keys == t, key_new == t], [idx, idx_new], need, nbits)
        sel = ((keys > t) | ((keys == t) & (idx < lim))) & (keys > KEY_NEG_INF)
        sel_new = ((key_new > t) | ((key_new == t) & (idx_new < lim))) & (key_new > KEY_NEG_INF)

        q = q_ref[...]
        m = jnp.full((QROWS, 1), NEG_INF, F32)
        l = jnp.zeros((QROWS, 1), F32)
        acc = jnp.zeros((QROWS, hd), F32)
        for cc in range(nck):
            kv = kv_sc[cc]
            s = _dot(q, kv[:hd, :]) * scale + bias_ref[cc]
            m, l, p, alpha = _decode_softmax_step(s, sel[cc:cc + 1, :], m, l)
            acc = alpha * acc + _dot_nt(p, kv[hd:, :])
        s = _dot(q, new[:hd, :]) * scale + bias_ref[nck][:, :LANES]
        m, l, p, alpha = _decode_softmax_step(s, sel_new, m, l)
        acc = alpha * acc + _dot_nt(p, new[hd:2 * hd, :])
        o_ref[...] = acc / jnp.where(l > 0, l, 1.0)


def dsa_sample(cache, layer, page_table, dsa_q, q_idx, w_idx, new_rows, bias_keys):
    s_n, n_pages = page_table.shape
    npp = min(PAGES_PER_STEP, n_pages)
    nc = n_pages // npp
    n = npp * PAGE_ROWS
    total = n_pages * PAGE_ROWS + 1
    pad_h = lambda x: jnp.pad(x, ((0, 0), (0, QROWS - x.shape[1]), (0, 0)))
    w8 = jnp.broadcast_to(pad_h(w_idx[:, :, None]), (s_n, QROWS, LANES))
    new_pad = jnp.pad(new_rows[:, :, None], ((0, 0), (0, 0), (0, LANES - 1)))
    grid_spec = pltpu.PrefetchScalarGridSpec(
        num_scalar_prefetch=1, grid=(s_n, nc),
        in_specs=[pl.BlockSpec((None, QROWS, HEAD_DIM), lambda s, c, pt: (s, 0, 0)),
                  pl.BlockSpec((None, QROWS, D_IDX), lambda s, c, pt: (s, 0, 0)),
                  pl.BlockSpec((None, QROWS, LANES), lambda s, c, pt: (s, 0, 0)),
                  pl.BlockSpec((nc + 1, QROWS, n), lambda s, c, pt: (0, 0, 0)),
                  pl.BlockSpec((None, DSA_KV_W, LANES), lambda s, c, pt: (s, 0, 0))]
                 + _page_specs(DSA_KV_W, layer, npp, lambda c: c, transposed=True),
        out_specs=pl.BlockSpec((None, QROWS, HEAD_DIM), lambda s, c, pt: (s, 0, 0)),
        scratch_shapes=[pltpu.VMEM((nc, n), jnp.int32), pltpu.VMEM((nc, 2 * HEAD_DIM, n), MXU_DTYPE)])
    out = pl.pallas_call(
        functools.partial(_dsa_s_kernel, npp=npp, n_keep=min(TOPK_KEYS, total // 4)), grid_spec=grid_spec,
        out_shape=jax.ShapeDtypeStruct((s_n, QROWS, HEAD_DIM), F32),
        compiler_params=_cparams("parallel", "arbitrary"), name="dsa_sample",
    )(page_table, _mx(pad_h(dsa_q)), _mx(pad_h(q_idx)), w8, bias_keys, new_pad, *([cache] * npp))
    return out[:, :H_DSA].reshape(s_n, H_DSA * HEAD_DIM)


def _pool_kernel(x_ref, pe_ref, w_ref, o_ref):
    pp, page, _ = x_ref.shape
    per_page = page // CMP_STRIDE
    acc = [None, None]
    for t in range(CMP_STRIDE):
        xt = x_ref[:, pl.ds(t, per_page, stride=CMP_STRIDE), :].reshape(pp * per_page, 2 * HEAD_DIM)
        for half in range(2):
            y = _dot(xt + pe_ref[half, t:t + 1, :], w_ref[half, t])
            acc[half] = y if acc[half] is None else acc[half] + y
    o_ref[:, :, :2 * CMP_HIDDEN] = acc[0].reshape(pp, per_page, 2 * CMP_HIDDEN)
    o_ref[:, :, 2 * CMP_HIDDEN:] = acc[1].reshape(pp, per_page, 2 * CMP_HIDDEN)


def pool_compress_hidden(cache, layer, w1, pe):
    depth, n_pool, page, width = cache.shape
    per_page = page // CMP_STRIDE
    w1r = w1.reshape(2, 2, CMP_STRIDE, HEAD_DIM, CMP_HIDDEN)
    wz = jnp.zeros((2, CMP_STRIDE, 2 * HEAD_DIM, 2 * CMP_HIDDEN), F32)
    wz = _mx(wz.at[:, :, :HEAD_DIM, :CMP_HIDDEN].set(w1r[0]).at[:, :, HEAD_DIM:, CMP_HIDDEN:].set(w1r[1]))
    pez = jnp.transpose(pe.reshape(2, 2, CMP_STRIDE, HEAD_DIM), (1, 2, 0, 3)).reshape(2, CMP_STRIDE, 2 * HEAD_DIM)
    pp = next(c for c in (32, 16, 8, 4, 2, 1) if n_pool % c == 0)
    return pl.pallas_call(
        _pool_kernel,
        grid=(n_pool // pp,),
        in_specs=[pl.BlockSpec((None, pp, page, 2 * HEAD_DIM), lambda i: (layer, i, 0, 0)),
                  pl.BlockSpec((2, CMP_STRIDE, 2 * HEAD_DIM), lambda i: (0, 0, 0)),
                  pl.BlockSpec((2, CMP_STRIDE, 2 * HEAD_DIM, 2 * CMP_HIDDEN), lambda i: (0, 0, 0, 0))],
        out_specs=pl.BlockSpec((pp, per_page, 4 * CMP_HIDDEN), lambda i: (i, 0, 0)),
        out_shape=jax.ShapeDtypeStruct((n_pool, per_page, 4 * CMP_HIDDEN), F32),
        compiler_params=_cparams("parallel"), name="nsa_pool_hidden",
    )(cache, pez, wz)


def _nsa_s_kernel(pt_ref, q_ref, gate_ref, w2_ref, gk_ref, bc_ref, smat_ref, fold_ref, exp_ref, bs_ref, win_ref,
                  bw_ref, newn_ref, neww_ref, *rest, npp, past_len, n_sel):
    ab_pages = rest[:npp]
    ab_next = rest[npp]
    pages = rest[npp + 1:2 * npp + 1]
    o_ref, sc_sc, vc_sc, kv_sc, r_sc = rest[2 * npp + 1:]
    c = pl.program_id(1)
    nc = pl.num_programs(1)
    hd = HEAD_DIM
    scale = hd ** -0.5
    q = q_ref[...]
    ncb = npp * PAGE_ROWS // CMP_STRIDE

    ab = jnp.concatenate([a[...] for a in ab_pages], axis=0)
    second = ab[:, 2 * CMP_HIDDEN:]
    row = lax.broadcasted_iota(jnp.int32, (ncb, 1), 0)
    second = jnp.where(row == ncb - 1, ab_next[0:1, 2 * CMP_HIDDEN:], jnp.roll(second, -1, axis=0))
    hid = jax.nn.gelu(ab[:, :2 * CMP_HIDDEN] + second)
    kc = _dot(hid[:, :CMP_HIDDEN], w2_ref[0])
    kc = kc * lax.rsqrt(jnp.mean(kc * kc, axis=-1, keepdims=True) + EPS) * gk_ref[...]
    cmp_end = (c * ncb + lax.broadcasted_iota(jnp.int32, (1, ncb), 1)) * CMP_STRIDE + (CMP_BLOCK - 1)
    sc_sc[c] = jnp.where(cmp_end <= past_len, _dot_nt(q, kc) * scale + bc_ref[c], NEG_INF)
    vc_sc[c] = _dot(hid[:, CMP_HIDDEN:], w2_ref[1])
    kv_sc[c] = jnp.concatenate([_mx(pg[:, 2 * hd:]) for pg in pages], axis=0)

    @pl.when(c == nc - 1)
    def _():
        nck = sc_sc.shape[0]
        n = npp * PAGE_ROWS
        s = jnp.concatenate([sc_sc[cc] for cc in range(nck)], axis=1)
        m = jnp.max(s, axis=1, keepdims=True)
        e = jnp.exp(s - jnp.where(m == NEG_INF, 0.0, m))
        d = jnp.sum(e, axis=1, keepdims=True)
        pc = e / jnp.where(d > 0, d, 1.0)
        o_c = jnp.zeros((QROWS, hd), F32)
        for cc in range(nck):
            o_c = o_c + _dot(pc[:, cc * ncb:(cc + 1) * ncb], vc_sc[cc])
        imp = ((pc[0:1] + pc[1:2]) + pc[2:3]) + pc[3:4]
        score = _exact_dot(jnp.broadcast_to(imp, pc.shape), smat_ref[...])
        nsp = score.shape[1]
        blk = lax.broadcasted_iota(jnp.int32, (QROWS, nsp), 1)
        back = past_len // SLC_BLOCK - blk
        forced = (blk == 0) | ((back >= 0) & (back < N_LOCAL))
        score = jnp.where(back >= 0, jnp.where(forced, jnp.inf, score), NEG_INF)
        sel_f = jnp.where(_topk_mask_row(score, n_sel), 1.0, 0.0)
        bpc = n // SLC_BLOCK
        nrow = r_sc.shape[0]
        chunk_of_blk = lax.broadcasted_iota(jnp.int32, (nrow, nsp), 1) // bpc
        own = chunk_of_blk == lax.broadcasted_iota(jnp.int32, (nrow, nsp), 0)
        sel_rows = jnp.concatenate([sel_f] * (nrow // QROWS), axis=0)
        r_sc[...] = _dot(jnp.where(own, sel_rows, 0.0), fold_ref[...])

        st = (jnp.full((QROWS, 1), NEG_INF, F32), jnp.zeros((QROWS, 1), F32))
        acc = jnp.zeros((QROWS, hd), F32)
        for cc in range(nck):
            picked = _dot(jnp.broadcast_to(r_sc[cc:cc + 1, :], (QROWS, LANES)), exp_ref[...]) > 0.5
            kv = kv_sc[cc]
            sc = _dot_nt(q, kv[:, :hd]) * scale + bs_ref[cc]
            m2, l2, p, alpha = _decode_softmax_step(sc, picked, *st)
            st = (m2, l2)
            acc = alpha * acc + _dot(p, kv[:, hd:])
        lane_n = lax.broadcasted_iota(jnp.int32, (QROWS, LANES), 1)
        newn = _mx(newn_ref[...])
        nb_row, nb_lane = (past_len // SLC_BLOCK) // bpc, (past_len // SLC_BLOCK) % bpc
        own_blk = jnp.sum(jnp.where(lane_n[0:1] == nb_lane, r_sc[nb_row:nb_row + 1, :], 0.0), axis=1, keepdims=True)
        sc = _dot_nt(q, newn[:, 2 * hd:3 * hd]) * scale + bs_ref[nck][:, :LANES]
        m2, l2, p, alpha = _decode_softmax_step(sc, (lane_n == 0) & (own_blk > 0.5), *st)
        acc = alpha * acc + _dot(p, newn[:, 3 * hd:])
        o_s = acc / jnp.where(l2 > 0, l2, 1.0)

        win = _mx(win_ref[...])
        wl = win.shape[0]
        dist = wl - lax.broadcasted_iota(jnp.int32, (QROWS, wl), 1)
        sc = _dot_nt(q, win[:, :hd]) * scale + bw_ref[:, :wl]
        st = (jnp.full((QROWS, 1), NEG_INF, F32), jnp.zeros((QROWS, 1), F32))
        m3, l3, p, alpha = _decode_softmax_step(sc, dist < WINDOW, *st)
        acc = _dot(p, win[:, hd:])
        neww = _mx(neww_ref[...])
        sc = _dot_nt(q, neww[:, :hd]) * scale + bw_ref[:, wl:wl + LANES]
        m3, l3, p, alpha = _decode_softmax_step(sc, lane_n == 0, m3, l3)
        acc = alpha * acc + _dot(p, neww[:, hd:])
        o_w = acc / jnp.where(l3 > 0, l3, 1.0)

        g = gate_ref[...]
        o_ref[...] = g[:, 0:1] * o_c + g[:, 1:2] * o_s + g[:, 2:3] * o_w


def nsa_sample(cache, layer, page_table, hidden_pool, nsa_q, nsa_g, new_nsa, new_win, win_state, w2, g_k0,
               bias_cmp, bias_keys, bias_win, smat):
    s_n, n_pages = page_table.shape
    npp = min(PAGES_PER_STEP, n_pages)
    nc = n_pages // npp
    n = npp * PAGE_ROWS
    past_len = n_pages * PAGE_ROWS
    ncb = n // CMP_STRIDE
    per_page = PAGE_ROWS // CMP_STRIDE
    bpc = n // SLC_BLOCK
    nsp = smat.shape[1]
    nrow = -(-(nsp // bpc + 1) // QROWS) * QROWS
    pad_h = lambda x: jnp.pad(x, ((0, 0), (0, QROWS - x.shape[1]), (0, 0)))
    gate8 = jnp.pad(pad_h(nsa_g), ((0, 0), (0, 0), (0, LANES - 3)))
    pad_row = lambda x: jnp.pad(x[:, None, :], ((0, 0), (0, LANES - 1), (0, 0)))
    fold = (lax.broadcasted_iota(jnp.int32, (nsp, LANES), 0) % bpc
            == lax.broadcasted_iota(jnp.int32, (nsp, LANES), 1)).astype(MXU_DTYPE)
    expand = (lax.broadcasted_iota(jnp.int32, (LANES, n), 0)
              == lax.broadcasted_iota(jnp.int32, (LANES, n), 1) // SLC_BLOCK).astype(MXU_DTYPE)
    wl = win_state.shape[1]
    const = lambda shape: pl.BlockSpec(shape, lambda s, c, pt: (0,) * len(shape))
    ab_spec = lambda n_: pl.BlockSpec((None, per_page, 4 * CMP_HIDDEN), lambda s, c, pt: (pt[s, c * npp + n_], 0, 0))
    grid_spec = pltpu.PrefetchScalarGridSpec(
        num_scalar_prefetch=1, grid=(s_n, nc),
        in_specs=[pl.BlockSpec((None, QROWS, HEAD_DIM), lambda s, c, pt: (s, 0, 0)),
                  pl.BlockSpec((None, QROWS, LANES), lambda s, c, pt: (s, 0, 0)),
                  const((2, CMP_HIDDEN, HEAD_DIM)), const((1, HEAD_DIM)),
                  const((nc, QROWS, ncb)), const(smat.shape), const((nsp, LANES)), const((LANES, n)),
                  const((nc + 1, QROWS, n)),
                  pl.BlockSpec((None, wl, WIN_KV_W), lambda s, c, pt: (s, 0, 0)),
                  const((QROWS, wl + LANES)),
                  pl.BlockSpec((None, LANES, NSA_KV_W), lambda s, c, pt: (s, 0, 0)),
                  pl.BlockSpec((None, LANES, WIN_KV_W), lambda s, c, pt: (s, 0, 0))]
                 + [ab_spec(n_) for n_ in range(npp)]
                 + [pl.BlockSpec((None, per_page, 4 * CMP_HIDDEN),
                                 lambda s, c, pt: (pt[s, jnp.minimum((c + 1) * npp, n_pages - 1)], 0, 0))]
                 + _page_specs(NSA_KV_W, layer, npp, lambda c: c),
        out_specs=pl.BlockSpec((None, QROWS, HEAD_DIM), lambda s, c, pt: (s, 0, 0)),
        scratch_shapes=[pltpu.VMEM((nc, QROWS, ncb), F32), pltpu.VMEM((nc, ncb, HEAD_DIM), F32),
                        pltpu.VMEM((nc, n, 2 * HEAD_DIM), MXU_DTYPE), pltpu.VMEM((nrow, LANES), F32)])
    out = pl.pallas_call(
        functools.partial(_nsa_s_kernel, npp=npp, past_len=past_len,
                          n_sel=min(N_SELECT, -(-(past_len + 1) // SLC_BLOCK))),
        grid_spec=grid_spec,
        out_shape=jax.ShapeDtypeStruct((s_n, QROWS, HEAD_DIM), F32),
        compiler_params=_cparams("parallel", "arbitrary"), name="nsa_sample",
    )(page_table, _mx(pad_h(nsa_q)), gate8, _mx(w2), g_k0.reshape(1, HEAD_DIM), bias_cmp, smat, fold, expand,
      bias_keys, win_state, bias_win, pad_row(new_nsa), pad_row(new_win),
      *([hidden_pool] * (npp + 1)), *([cache] * npp))
    return out[:, :H_NSA].reshape(s_n, H_NSA * HEAD_DIM)


def rmsnorm(x, g):
    xf = x.astype(jnp.float32)
    y = xf * lax.rsqrt(jnp.mean(xf * xf, axis=-1, keepdims=True) + EPS)
    return y.astype(x.dtype) * g


def rope(x, pos):
    half = x.shape[-1] // 2
    freqs = ROPE_THETA ** (-jnp.arange(half, dtype=jnp.float32) / half)
    ang = pos.astype(jnp.float32)[..., None] * freqs
    cos, sin = jnp.cos(ang), jnp.sin(ang)
    xf = x.astype(jnp.float32)
    x1, x2 = xf[..., :half], xf[..., half:]
    return jnp.concatenate([x1 * cos - x2 * sin, x2 * cos + x1 * sin], axis=-1).astype(x.dtype)


def t5_bias(table, dist):
    n = jnp.maximum(dist, 0)
    exact = NUM_BUCKETS // 2
    big = exact + (jnp.log(jnp.maximum(n, exact).astype(jnp.float32) / exact)
                   / math.log(T5_MAX_DIST / exact) * (NUM_BUCKETS - exact)).astype(jnp.int32)
    bucket = jnp.where(n < exact, n, jnp.minimum(big, NUM_BUCKETS - 1))
    onehot = (bucket[..., None] == jnp.arange(NUM_BUCKETS, dtype=jnp.int32)).astype(jnp.float32)
    looked_up = jnp.einsum('...b,bh->...h', onehot, table.astype(jnp.float32), precision=lax.Precision.HIGHEST)
    return jnp.moveaxis(looked_up, -1, 0)


def group_transforms(y, pos, p):
    B, L, _ = y.shape
    hd = HEAD_DIM
    seg = lambda c, w: y[..., c:c + w]
    mqa, mkv = seg(C_MQA, Q_LORA), seg(C_MKV, MLA_KV_W)
    nkv, dkv = seg(C_NKV, NSA_KV_W + WIN_KV_W), seg(C_DKV, 2 * hd)
    q = lin(mqa.reshape(B * L, Q_LORA), p['w_mla_uq'], g=p['g_mla_qa'], pro="rms",
            tm=min(512, B * L)).reshape(B, L, H_MLA, DN_MLA + DR_MLA)
    gq = p['g_mla_q']
    mla_q = jnp.concatenate([rmsnorm(q[..., :DN_MLA], gq[:DN_MLA]),
                             rope(rmsnorm(q[..., DN_MLA:], gq[DN_MLA:]), pos[:, None])], axis=-1)
    mla_rows = jnp.concatenate([rmsnorm(mkv[..., :KV_LORA], p['g_mla_kva']),
                                rope(rmsnorm(mkv[..., KV_LORA:], p['g_mla_k'][DN_MLA:]), pos)], axis=-1)
    sb_q = seg(C_SQ, H_SB * hd).reshape(B, L, H_SB, hd)
    nsa_q = rmsnorm(seg(C_NQ, H_NSA * hd).reshape(B, L, H_NSA, hd), p['g_nsa_q'])
    gk = p['g_nsa_k']
    nsa_rows = jnp.concatenate([nkv[..., :2 * hd], rmsnorm(nkv[..., 2 * hd:3 * hd], gk[1]), nkv[..., 3 * hd:4 * hd]], axis=-1)
    win_rows = jnp.concatenate([rmsnorm(nkv[..., 4 * hd:5 * hd], gk[2]), nkv[..., 5 * hd:]], axis=-1)
    nsa_g = jax.nn.sigmoid(seg(C_NG, 3 * H_NSA).reshape(B, L, H_NSA, 3))
    dsa_q = rmsnorm(seg(C_DQ, H_DSA * hd).reshape(B, L, H_DSA, hd), p['g_dsa_q'])
    dsa_rows = jnp.concatenate([rmsnorm(dkv[..., :hd], p['g_dsa_k']), dkv[..., hd:], seg(C_DIK, D_IDX)], axis=-1)
    q_idx = seg(C_DIQ, H_IDX * D_IDX).reshape(B, L, H_IDX, D_IDX)
    qs = (mla_q, sb_q, nsa_q, nsa_g, dsa_q, q_idx, seg(C_DIW, H_IDX))
    rows = (mla_rows, seg(C_SKV, SB_KV_W), nsa_rows, win_rows, dsa_rows)
    return qs, rows


def bias_tiles(table):
    r = jnp.arange(TILE, dtype=jnp.int32)
    tiles = [t5_bias(table, d * TILE + r[:, None] - r[None, :]) for d in (0, 1)]
    far = t5_bias(table, jnp.full((TILE, TILE), 2 * TILE + T5_MAX_DIST, jnp.int32))
    return jnp.stack(tiles + [far], axis=0)


def selection_matrix(ncp, n_cmp, n_slc, nsp):
    ratio = SLC_BLOCK // CMP_STRIDE
    span = CMP_BLOCK // CMP_STRIDE
    s = np.zeros((ncp, nsp), np.float32)
    for b in range(n_slc):
        for m in range(ratio):
            for n in range(span):
                c = ratio * b + m + n - (span - 1)
                if 0 <= c < n_cmp:
                    s[c, b] += 1.0
    return jnp.asarray(s, MXU_DTYPE)


def prompt_mixers(qs, rows, p, tb_nsa, tb_dsa, bias_c, smat):
    mla_q, sb_q, nsa_q, nsa_g, dsa_q, q_idx, w_idx = qs
    mla_r, sb_r, nsa_r, win_r, dsa_r = rows
    B, L = mla_r.shape[:2]
    hd = HEAD_DIM
    heads_first = lambda x: jnp.transpose(x, (0, 2, 1, 3))
    kv = lin(mla_r[..., :KV_LORA].reshape(B * L, KV_LORA), p['w_mla_ukv'], tm=min(512, B * L))
    kv = kv.reshape(B, L, H_MLA, DN_MLA + DV_MLA)
    k_n = rmsnorm(kv[..., :DN_MLA], p['g_mla_k'][:DN_MLA])
    k_r = jnp.broadcast_to(mla_r[:, :, None, KV_LORA:], (B, L, H_MLA, DR_MLA))
    zpad = jnp.zeros((B, L, H_MLA, LANES - DN_MLA - DR_MLA), F32)
    k_pack = _mx(jnp.concatenate([k_n, k_r, zpad], axis=-1)).reshape(B, L, H_MLA * LANES)
    q_pack = _mx(jnp.concatenate([mla_q, zpad], axis=-1)).reshape(B, L, H_MLA * LANES)
    v_pack = _mx(kv[..., DN_MLA:]).reshape(B, L, H_MLA * DV_MLA)
    o_mla = mla_prompt(q_pack, k_pack, v_pack)
    o_sb = sb_prompt(heads_first(sb_q), sb_r[..., :hd], sb_r[..., hd:])
    n_chunk = L // CMP_STRIDE
    n_cmp = n_chunk - 1
    ncp = smat.shape[0]
    raw = jnp.stack([nsa_r[..., :hd], nsa_r[..., hd:2 * hd]], axis=1).reshape(B, 2, n_chunk, CMP_STRIDE * hd)
    blocks = jnp.concatenate([raw[:, :, :-1], raw[:, :, 1:]], axis=-1) + p['pe_cmp'].reshape(1, 2, 1, CMP_BLOCK * hd)
    blocks = jnp.pad(blocks, ((0, 0), (0, 0), (0, ncp - n_cmp), (0, 0)))
    cmp_kv = compress(blocks, p['w_cmp1'], p['w_cmp2'], p['g_nsa_k'][0])
    o_nsa = nsa_prompt(heads_first(nsa_q), heads_first(nsa_g), cmp_kv, bias_c,
                       nsa_r[..., 2 * hd:3 * hd], nsa_r[..., 3 * hd:], win_r[..., :hd], win_r[..., hd:], smat, tb_nsa)
    o_dsa = dsa_prompt(heads_first(dsa_q), heads_first(q_idx), w_idx, dsa_r[..., :hd], dsa_r[..., hd:2 * hd],
                       dsa_r[..., 2 * hd:], tb_dsa)
    return tuple(o.reshape(B * L, BR_WIDTH) for o in (o_mla, o_sb, o_nsa, o_dsa))


def pack_w_in(w):
    cuts = np.cumsum((0,) + IN_SIZES)
    cols = []
    for i in range(10):
        cols.append(jnp.pad(w[:, cuts[i]:cuts[i + 1]], ((0, 0), (0, _SEG_PAD[i] - IN_SIZES[i]))))
    tail = w[:, cuts[10]:cuts[12]]
    cols.append(jnp.pad(tail, ((0, 0), (0, N_SMALL - C_DIK - tail.shape[1]))))
    cols.append(w[:, cuts[12]:])
    return _mx(jnp.concatenate(cols, axis=1))


def kernel(x_prompt, x_sample, cache_mla, cache_sb, cache_nsa, cache_dsa, state_nsa_win, page_table,
           c_prompt, c_sample, w_ada, b_ada, g_norm_mix, g_norm_ffn, w_in, g_mla_qa, w_mla_uq, g_mla_kva,
           w_mla_ukv, g_mla_q, g_mla_k, w_cmp1, w_cmp2, pe_cmp, g_nsa_q, g_nsa_k, g_dsa_q, g_dsa_k, rel_bias,
           w_br, w_out, w_router, b_router, w_gu, b_gu, w_down, b_down):
    B, L, D = x_prompt.shape
    S, seq_s, _ = x_sample.shape
    past_len = page_table.shape[1] * cache_mla.shape[2]
    w_len = state_nsa_win.shape[2]
    w_len_p = min(WINDOW, L)
    pos_p = jnp.arange(L, dtype=jnp.int32)
    pos_s = past_len + jnp.arange(seq_s, dtype=jnp.int32)
    tm_p = min(256, L)
    tpg_p = L // tm_p
    ts = S * seq_s

    tb_nsa = bias_tiles(rel_bias[:, :H_NSA])
    tb_dsa = bias_tiles(rel_bias[:, H_NSA:])
    n_cmp = L // CMP_STRIDE - 1
    ncp = -(-n_cmp // LANES) * LANES
    n_slc = -(-L // SLC_BLOCK)
    nsp = -(-n_slc // LANES) * LANES
    cmp_end = jnp.arange(ncp, dtype=jnp.int32) * CMP_STRIDE + (CMP_BLOCK - 1)
    bias_c = t5_bias(rel_bias[:, :H_NSA], pos_p[:, None] - cmp_end[None, :])
    smat = selection_matrix(ncp, n_cmp, n_slc, nsp)

    assert seq_s == 1
    n_pages = page_table.shape[1]
    npp = min(PAGES_PER_STEP, n_pages)
    nc_s = n_pages // npp
    n_s = npp * PAGE_ROWS
    pad_heads = lambda b: jnp.pad(b, [(0, 0)] * (b.ndim - 2) + [(0, QROWS - b.shape[-2]), (0, 0)])
    kpos = jnp.arange((nc_s + 1) * n_s, dtype=jnp.int32).reshape(nc_s + 1, n_s)
    by_key = lambda tab: pad_heads(jnp.transpose(t5_bias(tab, past_len - kpos), (1, 0, 2)))
    sb_nsa = by_key(rel_bias[:, :H_NSA])
    sb_dsa = by_key(rel_bias[:, H_NSA:])
    ncb_s = n_s // CMP_STRIDE
    cmp_end_s = jnp.arange(nc_s * ncb_s, dtype=jnp.int32).reshape(nc_s, ncb_s) * CMP_STRIDE + (CMP_BLOCK - 1)
    sb_cmp = pad_heads(jnp.transpose(t5_bias(rel_bias[:, :H_NSA], past_len - cmp_end_s), (1, 0, 2)))
    wdist = jnp.concatenate([w_len - jnp.arange(w_len, dtype=jnp.int32), -jnp.arange(LANES, dtype=jnp.int32)])
    sb_win = pad_heads(t5_bias(rel_bias[:, :H_NSA], wdist))
    n_slc_s = -(-(past_len + 1) // SLC_BLOCK)
    smat_s = selection_matrix(nc_s * ncb_s, (past_len + 1) // CMP_STRIDE - 1, n_slc_s, -(-n_slc_s // LANES) * LANES)

    src_col = jnp.arange(2 * D_FF, dtype=jnp.int32)
    src_col = jnp.where(src_col < D_FF, 2 * src_col, 2 * (src_col - D_FF) + 1)
    deinterleave = (jnp.arange(2 * D_FF, dtype=jnp.int32)[:, None] == src_col[None, :]).astype(MXU_DTYPE)

    cache_mla_t = jnp.swapaxes(cache_mla, 2, 3)
    cache_dsa_t = jnp.swapaxes(cache_dsa, 2, 3)

    xp = x_prompt.reshape(B * L, D)
    xs = x_sample.reshape(ts, D)
    c_all = jnp.concatenate([c_prompt, c_sample], axis=0)
    out_p = []
    out_s = []
    for l in range(DEPTH):
        p = {'g_mla_qa': g_mla_qa[l], 'w_mla_uq': w_mla_uq[l], 'g_mla_kva': g_mla_kva[l],
             'w_mla_ukv': w_mla_ukv[l], 'g_mla_q': g_mla_q[l], 'g_mla_k': g_mla_k[l], 'w_cmp1': w_cmp1[l],
             'w_cmp2': w_cmp2[l], 'pe_cmp': pe_cmp[l], 'g_nsa_q': g_nsa_q[l], 'g_nsa_k': g_nsa_k[l],
             'g_dsa_q': g_dsa_q[l], 'g_dsa_k': g_dsa_k[l]}
        mod = lin(c_all, w_ada[l], b=b_ada[l], pro="silu", tn=6 * D // 4)
        mod_p = [m.reshape(B, 1, D) for m in jnp.split(mod[:B], 6, axis=-1)]
        mod_s = [m.reshape(1, ts, D) for m in jnp.split(mod[B:], 6, axis=-1)]
        w_p = pack_w_in(w_in[l])

        y_p, gates_p = proj_in(xp, g_norm_mix[l], mod_p[0], mod_p[1], w_p, tm_p, tpg_p)
        qs_p, rows_p = group_transforms(y_p.reshape(B, L, N_SMALL), pos_p, p)
        o4_p = prompt_mixers(qs_p, rows_p, p, tb_nsa, tb_dsa, bias_c, smat)
        xp = merge(o4_p, gates_p, w_br[l], w_out[l], xp, mod_p[2], tm_p, tpg_p)

        y_s, gates_s = proj_in(xs, g_norm_mix[l], mod_s[0], mod_s[1], w_p, ts, 1)
        qs_s, rows_s = group_transforms(y_s.reshape(S, seq_s, N_SMALL), pos_s, p)

        sq = [a[:, 0] for a in qs_s]
        sr = [a[:, 0] for a in rows_s]
        hidden_pool = pool_compress_hidden(cache_nsa, l, w_cmp1[l], pe_cmp[l])
        o4_s = (mla_sample(cache_mla_t, l, page_table, sq[0], sr[0], w_mla_ukv[l], g_mla_k[l][:DN_MLA]),
                sb_sample(cache_sb, l, page_table, sq[1]),
                nsa_sample(cache_nsa, l, page_table, hidden_pool, sq[2], sq[3], sr[2], sr[3], state_nsa_win[l],
                           w_cmp2[l], g_nsa_k[l][0], sb_cmp, sb_nsa, sb_win, smat_s),
                dsa_sample(cache_dsa_t, l, page_table, sq[4], sq[5], sq[6], sr[4], sb_dsa))
        xs = merge(o4_s, gates_s, w_br[l], w_out[l], xs, mod_s[2], ts, 1)

        h_p, lg_p = router(xp, g_norm_ffn[l], mod_p[3], mod_p[4], w_router[l], b_router[l], tm_p, tpg_p)
        h_s, lg_s = router(xs, g_norm_ffn[l], mod_s[3], mod_s[4], w_router[l], b_router[l], ts, 1)
        wg = lin(w_gu[l].reshape(N_EXPERTS * D, 2 * D_FF), deinterleave, tm=1024,
                 out_dtype=MXU_DTYPE).reshape(N_EXPERTS, D, 2 * D_FF)
        bg = jnp.concatenate([b_gu[l][:, 0::2], b_gu[l][:, 1::2]], axis=-1)[:, None, :]
        y_all = moe_ffn(jnp.concatenate([h_p, h_s], axis=0),
                        jnp.concatenate([lg_p, lg_s], axis=0)[:, :N_EXPERTS],
                        wg, bg, _mx(w_down[l]), b_down[l][:, None, :])
        xp = xp + jnp.broadcast_to(mod_p[5], (B, L, D)).reshape(B * L, D) * y_all[:B * L]
        xs = xs + mod_s[5].reshape(ts, D) * y_all[B * L:]

        out_p.append((rows_p[0], rows_p[1], rows_p[2], rows_p[4], rows_p[3][:, L - w_len_p:]))
        win_s = jnp.concatenate([state_nsa_win[l], rows_s[3]], axis=1)
        out_s.append((rows_s[0], rows_s[1], rows_s[2], rows_s[4], win_s[:, win_s.shape[1] - w_len:]))
    new_p = [jnp.stack([o[i] for o in out_p], axis=0) for i in range(5)]
    new_s = [jnp.stack([o[i] for o in out_s], axis=0) for i in range(5)]
    return (xp.reshape(B, L, D), xs.reshape(S, seq_s, D), *new_p, *new_s)
```

```python
import functools
import math
import jax, jax.numpy as jnp
from jax import lax
import numpy as np
from jax.experimental import pallas as pl
from jax.experimental.pallas import tpu as pltpu

D_MODEL = 1024
DEPTH = 2
HEAD_DIM = 64
H_MLA = 4
Q_LORA = 256
KV_LORA = 128
DN_MLA = 64
DR_MLA = 32
DV_MLA = 64
ROPE_THETA = 10000.0
H_SB = 4
H_NSA = 4
CMP_STRIDE = 16
CMP_BLOCK = 2 * CMP_STRIDE
CMP_HIDDEN = 128
SLC_BLOCK = 64
N_SELECT = 16
N_LOCAL = 2
WINDOW = 512
H_DSA = 4
H_IDX = 4
D_IDX = 32
TOPK_KEYS = 256
NUM_BUCKETS = 32
T5_MAX_DIST = 128
N_EXPERTS = 32
TOP_K = 4
D_FF = 512
SWIGLU_LIMIT = 7.0
SWIGLU_ALPHA = 1.702
MOE_BLOCK = 128
QBLOCK = 128
N_BRANCH = 4
BR_WIDTH = 256
EPS = 1e-6
MLA_KV_W = KV_LORA + DR_MLA
SB_KV_W = 2 * HEAD_DIM
NSA_KV_W = 4 * HEAD_DIM
WIN_KV_W = 2 * HEAD_DIM
DSA_KV_W = 2 * HEAD_DIM + D_IDX
IN_SIZES = (Q_LORA, MLA_KV_W, H_SB * HEAD_DIM, SB_KV_W, H_NSA * HEAD_DIM, NSA_KV_W + WIN_KV_W, 3 * H_NSA,
            H_DSA * HEAD_DIM, 2 * HEAD_DIM, H_IDX * D_IDX, D_IDX, H_IDX, N_BRANCH * D_MODEL)

MXU_DTYPE = jnp.bfloat16
F32 = jnp.float32
NEG_INF = float("-inf")
LANES = 128
TILE = LANES
VMEM_LIMIT_BYTES = 56 * 1024 * 1024
MOE_ROWS = 256
INT_MIN = -2 ** 31
HEAD_GROUPS = 2
SB_DEAD_LOG = -110.0

_SEG_PAD = (256, 256, 256, 128, 256, 384, 128, 256, 128, 128, 0, 0)
C_MQA, C_MKV, C_SQ, C_SKV, C_NQ, C_NKV, C_NG, C_DQ, C_DKV, C_DIQ, C_DIK = (
    0, 256, 512, 768, 896, 1152, 1536, 1664, 1920, 2048, 2176)
C_DIW = C_DIK + D_IDX
N_SMALL = 2304


def _mx(a):
    return a.astype(MXU_DTYPE)


def _dot(a, b):
    return jnp.dot(_mx(a), _mx(b), preferred_element_type=F32)


def _dot_nt(a, b):
    return lax.dot_general(_mx(a), _mx(b), (((1,), (1,)), ((), ())), preferred_element_type=F32)


def _exact_dot(a, b, left=True):
    b = b.astype(MXU_DTYPE)
    f = (lambda u: jnp.dot(u, b, preferred_element_type=F32)) if left else (
        lambda u: jnp.dot(b, u, preferred_element_type=F32))
    if MXU_DTYPE == F32:
        return f(a)
    hi = a.astype(MXU_DTYPE)
    r1 = a - hi.astype(F32)
    mid = r1.astype(MXU_DTYPE)
    lo = (r1 - mid.astype(F32)).astype(MXU_DTYPE)
    return f(hi) + f(mid) + f(lo)


def _cparams(*sem):
    return pltpu.CompilerParams(dimension_semantics=sem, vmem_limit_bytes=VMEM_LIMIT_BYTES)


def _adaln_rows(x, g, sh, sc):
    y = x * lax.rsqrt(jnp.mean(x * x, axis=-1, keepdims=True) + EPS)
    return y * g * (1.0 + sc) + sh


def _proj_in_kernel(x_ref, g_ref, sh_ref, sc_ref, w_ref, ys_ref, gate_ref):
    h = _mx(_adaln_rows(x_ref[...], g_ref[...], sh_ref[...], sc_ref[...]))
    ys_ref[...] = jnp.dot(h, w_ref[:, :N_SMALL], preferred_element_type=F32)
    for c in range(N_BRANCH):
        lo = N_SMALL + c * D_MODEL
        gate_ref[:, c * D_MODEL:(c + 1) * D_MODEL] = jax.nn.sigmoid(
            jnp.dot(h, w_ref[:, lo:lo + D_MODEL], preferred_element_type=F32))


def _mod_specs(mod, tm, tiles_per_group):
    r = mod.shape[1]
    return pl.BlockSpec((None, r, mod.shape[2]), lambda i: (i // tiles_per_group, 0, 0))


def proj_in(x2d, g, sh, sc, w_p, tm, tiles_per_group):
    t, d = x2d.shape
    n = w_p.shape[1]
    return pl.pallas_call(
        _proj_in_kernel,
        grid=(t // tm,),
        in_specs=[pl.BlockSpec((tm, d), lambda i: (i, 0)),
                  pl.BlockSpec((1, d), lambda i: (0, 0)),
                  _mod_specs(sh, tm, tiles_per_group), _mod_specs(sc, tm, tiles_per_group),
                  pl.BlockSpec((d, n), lambda i: (0, 0))],
        out_specs=[pl.BlockSpec((tm, N_SMALL), lambda i: (i, 0)),
                   pl.BlockSpec((tm, N_BRANCH * D_MODEL), lambda i: (i, 0))],
        out_shape=[jax.ShapeDtypeStruct((t, N_SMALL), F32), jax.ShapeDtypeStruct((t, N_BRANCH * D_MODEL), F32)],
        compiler_params=_cparams("parallel"),
        name="proj_in",
    )(x2d, g.reshape(1, d), sh, sc, w_p)


def _lin_kernel(*refs, pro, has_g, has_b):
    refs = list(refs)
    x_ref = refs.pop(0)
    g_ref = refs.pop(0) if has_g else None
    w_ref = refs.pop(0)
    b_ref = refs.pop(0) if has_b else None
    o_ref = refs.pop(0)
    x = x_ref[...]
    if pro == "silu":
        x = x * jax.nn.sigmoid(x)
    elif pro == "rms":
        x = x * lax.rsqrt(jnp.mean(x * x, axis=-1, keepdims=True) + EPS) * g_ref[...]
    y = _dot(x, w_ref[...])
    if has_b:
        y = y + b_ref[...]
    o_ref[...] = y.astype(o_ref.dtype)


def lin(x, w, g=None, b=None, pro="none", tm=512, tn=None, out_dtype=F32):
    m, k = x.shape
    n = w.shape[1]
    tm = min(tm, m)
    tn = n if tn is None else tn
    args = [x]
    specs = [pl.BlockSpec((tm, k), lambda i, j: (i, 0))]
    if g is not None:
        args.append(g.reshape(1, k))
        specs.append(pl.BlockSpec((1, k), lambda i, j: (0, 0)))
    args.append(_mx(w))
    specs.append(pl.BlockSpec((k, tn), lambda i, j: (0, j)))
    if b is not None:
        args.append(b.reshape(1, n))
        specs.append(pl.BlockSpec((1, tn), lambda i, j: (0, j)))
    return pl.pallas_call(
        functools.partial(_lin_kernel, pro=pro, has_g=g is not None, has_b=b is not None),
        grid=(m // tm, n // tn),
        in_specs=specs,
        out_specs=pl.BlockSpec((tm, tn), lambda i, j: (i, j)),
        out_shape=jax.ShapeDtypeStruct((m, n), out_dtype),
        compiler_params=_cparams("parallel", "parallel"),
        name="lin_" + pro,
    )(*args)


def _router_kernel(x_ref, g_ref, sh_ref, sc_ref, w_ref, b_ref, h_ref, lg_ref):
    h = _adaln_rows(x_ref[...], g_ref[...], sh_ref[...], sc_ref[...])
    h_ref[...] = h.astype(h_ref.dtype)
    lg_ref[...] = _dot(h, w_ref[...]) + b_ref[...]


def router(x2d, g, sh, sc, w_r, b_r, tm, tiles_per_group):
    t, d = x2d.shape
    wp = jnp.pad(_mx(w_r), ((0, 0), (0, LANES - N_EXPERTS)))
    bp = jnp.pad(b_r.reshape(1, -1), ((0, 0), (0, LANES - N_EXPERTS)), constant_values=NEG_INF)
    return pl.pallas_call(
        _router_kernel,
        grid=(t // tm,),
        in_specs=[pl.BlockSpec((tm, d), lambda i: (i, 0)),
                  pl.BlockSpec((1, d), lambda i: (0, 0)),
                  _mod_specs(sh, tm, tiles_per_group), _mod_specs(sc, tm, tiles_per_group),
                  pl.BlockSpec((d, LANES), lambda i: (0, 0)),
                  pl.BlockSpec((1, LANES), lambda i: (0, 0))],
        out_specs=[pl.BlockSpec((tm, d), lambda i: (i, 0)), pl.BlockSpec((tm, LANES), lambda i: (i, 0))],
        out_shape=[jax.ShapeDtypeStruct((t, d), MXU_DTYPE), jax.ShapeDtypeStruct((t, LANES), F32)],
        compiler_params=_cparams("parallel"),
        name="router",
    )(x2d, g.reshape(1, d), sh, sc, wp, bp)


def _moe_kernel(be_ref, x_ref, wg_ref, bg_ref, wd_ref, bd_ref, o_ref):
    gu = _dot(x_ref[...], wg_ref[...]) + bg_ref[...]
    glu = jnp.minimum(gu[:, :D_FF], SWIGLU_LIMIT)
    lin_ = jnp.clip(gu[:, D_FF:], -SWIGLU_LIMIT, SWIGLU_LIMIT)
    act = glu * jax.nn.sigmoid(SWIGLU_ALPHA * glu) * (lin_ + 1.0)
    o_ref[...] = _dot(act, wd_ref[...]) + bd_ref[...]


def moe_blocks(xg, blk_e, wg, bg, wd, bd):
    n, d = xg.shape
    nb = n // MOE_ROWS
    grid_spec = pltpu.PrefetchScalarGridSpec(
        num_scalar_prefetch=1,
        grid=(nb,),
        in_specs=[pl.BlockSpec((MOE_ROWS, d), lambda b, be: (b, 0)),
                  pl.BlockSpec((None, d, 2 * D_FF), lambda b, be: (be[b], 0, 0)),
                  pl.BlockSpec((None, 1, 2 * D_FF), lambda b, be: (be[b], 0, 0)),
                  pl.BlockSpec((None, D_FF, d), lambda b, be: (be[b], 0, 0)),
                  pl.BlockSpec((None, 1, d), lambda b, be: (be[b], 0, 0))],
        out_specs=pl.BlockSpec((MOE_ROWS, d), lambda b, be: (b, 0)))
    return pl.pallas_call(
        _moe_kernel, grid_spec=grid_spec,
        out_shape=jax.ShapeDtypeStruct((n, d), F32),
        compiler_params=_cparams("arbitrary"),
        name="moe_ffn",
    )(blk_e, xg, wg, bg, wd, bd)


def moe_ffn(h, logits, wg, bg, wd, bd):
    t, d = h.shape
    top_v, top_e = lax.top_k(logits, TOP_K)
    wts = jax.nn.softmax(top_v, axis=-1)
    a = t * TOP_K
    flat_e = top_e.reshape(-1)
    onehot = (flat_e[:, None] == jnp.arange(N_EXPERTS, dtype=jnp.int32)[None, :]).astype(jnp.int32)
    rank = jnp.sum((jnp.cumsum(onehot, axis=0) - onehot) * onehot, axis=1)
    counts = jnp.sum(onehot, axis=0)
    padded = (counts + MOE_ROWS - 1) // MOE_ROWS * MOE_ROWS
    pad_end = jnp.cumsum(padded)
    pad_start = pad_end - padded
    dest = (pad_start[flat_e] + rank).astype(jnp.int32)
    nb = -(-a // MOE_ROWS) + N_EXPERTS
    slot_tok = jnp.full((nb * MOE_ROWS,), t, jnp.int32).at[dest].set(jnp.arange(a, dtype=jnp.int32) // TOP_K)
    blk_e = jnp.minimum(jnp.searchsorted(pad_end, jnp.arange(nb, dtype=jnp.int32) * MOE_ROWS, side='right'),
                        N_EXPERTS - 1).astype(jnp.int32)
    hpad = jnp.concatenate([h, jnp.zeros((1, d), h.dtype)], axis=0)
    out = moe_blocks(hpad[slot_tok], blk_e, wg, bg, wd, bd)
    dest = dest.reshape(t, TOP_K)
    y = wts[:, 0:1] * out[dest[:, 0]]
    for k in range(1, TOP_K):
        y = y + wts[:, k:k + 1] * out[dest[:, k]]
    return y


def _merge_kernel(o0_ref, o1_ref, o2_ref, o3_ref, gates_ref, wbr_ref, wout_ref, x_ref, gm_ref, out_ref):
    acc = None
    for b, o_ref in enumerate((o0_ref, o1_ref, o2_ref, o3_ref)):
        t = gates_ref[:, b * D_MODEL:(b + 1) * D_MODEL] * _dot(o_ref[...], wbr_ref[b])
        acc = t if acc is None else acc + t
    out_ref[...] = x_ref[...] + gm_ref[...] * _dot(acc, wout_ref[...])


def merge(o4, gates, w_br, w_out, x2d, gm, tm, tiles_per_group):
    t, d = x2d.shape
    ospec = pl.BlockSpec((tm, BR_WIDTH), lambda i: (i, 0))
    return pl.pallas_call(
        _merge_kernel,
        grid=(t // tm,),
        in_specs=[ospec, ospec, ospec, ospec,
                  pl.BlockSpec((tm, N_BRANCH * d), lambda i: (i, 0)),
                  pl.BlockSpec((N_BRANCH, BR_WIDTH, d), lambda i: (0, 0, 0)),
                  pl.BlockSpec((d, d), lambda i: (0, 0)),
                  pl.BlockSpec((tm, d), lambda i: (i, 0)),
                  _mod_specs(gm, tm, tiles_per_group)],
        out_specs=pl.BlockSpec((tm, d), lambda i: (i, 0)),
        out_shape=jax.ShapeDtypeStruct((t, d), F32),
        compiler_params=_cparams("parallel"),
        name="merge",
    )(*o4, gates, _mx(w_br), _mx(w_out), x2d, gm)


def _order_key(x):
    bits = lax.bitcast_convert_type(x, jnp.int32)
    return jnp.where(bits >= 0, bits, bits ^ jnp.int32(0x7FFFFFFF))


KEY_NEG_INF = int(np.array(-np.inf, np.float32).view(np.int32)) ^ 0x7FFFFFFF
KEY_NEG_INF = KEY_NEG_INF - 2 ** 32 if KEY_NEG_INF >= 2 ** 31 else KEY_NEG_INF


def _kth_threshold(count_ge, k, rows):
    zero = jnp.zeros((rows, 1), jnp.int32)
    t = jnp.where(count_ge(zero) >= k, zero, jnp.full((rows, 1), INT_MIN, jnp.int32))

    def step(s, t):
        cand = t + (jnp.int32(1) << (30 - s))
        return jnp.where(count_ge(cand) >= k, cand, t)

    return lax.fori_loop(0, 31, step, t)


def _lower_tri(n):
    return (lax.broadcasted_iota(jnp.int32, (n, n), 0) < lax.broadcasted_iota(jnp.int32, (n, n), 1)).astype(MXU_DTYPE)


def _topk_mask_row(score, k):
    r, c = score.shape
    key = _order_key(score)

    def count_ge(t):
        return jnp.sum(jnp.where(key >= t, 1.0, 0.0), axis=1, keepdims=True)

    t = _kth_threshold(count_ge, float(k), r)
    gt = key > t
    eq = key == t
    need = float(k) - jnp.sum(jnp.where(gt, 1.0, 0.0), axis=1, keepdims=True)
    prefix = jnp.dot(jnp.where(eq, 1.0, 0.0).astype(MXU_DTYPE), _lower_tri(c), preferred_element_type=F32)
    return (gt | (eq & (prefix < need))) & (key > KEY_NEG_INF)


def _softmax_step(s, ok, m, l, acc, v):
    s = jnp.where(ok, s, NEG_INF)
    m_new = jnp.maximum(m, jnp.max(s, axis=1, keepdims=True))
    m_safe = jnp.where(m_new == NEG_INF, 0.0, m_new)
    p = jnp.exp(s - m_safe)
    alpha = jnp.exp(m - m_safe)
    return m_new, alpha * l + jnp.sum(p, axis=1, keepdims=True), alpha * acc + _dot(p, v)


def _tile_bias(tb_ref, d, heads):
    b = jnp.where(d == 0, tb_ref[0], jnp.where(d == 1, tb_ref[1], tb_ref[2]))
    return b.reshape(heads * TILE, TILE)


def _store_heads(o_ref, o, heads, rows, width):
    for h in range(heads):
        o_ref[:, h * width:(h + 1) * width] = o[h * rows:(h + 1) * rows]


def _mla_kernel(q_ref, k_ref, v_ref, o_ref, m_sc, l_sc, acc_sc, *, tq, tk, scale):
    i = pl.program_id(1)
    j = pl.program_id(2)

    @pl.when(j == 0)
    def _():
        m_sc[...] = jnp.full(m_sc.shape, NEG_INF, F32)
        l_sc[...] = jnp.zeros(l_sc.shape, F32)
        acc_sc[...] = jnp.zeros(acc_sc.shape, F32)

    @pl.when(j * tk < (i + 1) * tq)
    def _():
        qpos = i * tq + lax.broadcasted_iota(jnp.int32, (tq, tk), 0)
        kpos = j * tk + lax.broadcasted_iota(jnp.int32, (tq, tk), 1)
        causal = kpos <= qpos
        for h in range(H_MLA):
            s = _dot_nt(q_ref[:, h * LANES:(h + 1) * LANES], k_ref[:, h * LANES:(h + 1) * LANES]) * scale
            m, l, acc = _softmax_step(s, causal, m_sc[h], l_sc[h], acc_sc[h], v_ref[:, h * DV_MLA:(h + 1) * DV_MLA])
            m_sc[h] = m
            l_sc[h] = l
            acc_sc[h] = acc

    @pl.when(j == pl.num_programs(2) - 1)
    def _():
        for h in range(H_MLA):
            o_ref[:, h * DV_MLA:(h + 1) * DV_MLA] = acc_sc[h] / l_sc[h]


def mla_prompt(q, k, v):
    b, l, _ = q.shape
    tq = min(256, l)
    tk = min(512, l)
    last = lambda i: ((i + 1) * tq - 1) // tk
    return pl.pallas_call(
        functools.partial(_mla_kernel, tq=tq, tk=tk, scale=(DN_MLA + DR_MLA) ** -0.5),
        grid=(b, l // tq, l // tk),
        in_specs=[pl.BlockSpec((None, tq, H_MLA * LANES), lambda bb, i, j: (bb, i, 0)),
                  pl.BlockSpec((None, tk, H_MLA * LANES), lambda bb, i, j: (bb, jnp.minimum(j, last(i)), 0)),
                  pl.BlockSpec((None, tk, H_MLA * DV_MLA), lambda bb, i, j: (bb, jnp.minimum(j, last(i)), 0))],
        out_specs=pl.BlockSpec((None, tq, H_MLA * DV_MLA), lambda bb, i, j: (bb, i, 0)),
        out_shape=jax.ShapeDtypeStruct((b, l, H_MLA * DV_MLA), F32),
        scratch_shapes=[pltpu.VMEM((H_MLA, tq, 1), F32), pltpu.VMEM((H_MLA, tq, 1), F32),
                        pltpu.VMEM((H_MLA, tq, DV_MLA), F32)],
        compiler_params=_cparams("parallel", "parallel", "arbitrary"),
        name="mla_prompt",
    )(q, k, v)


def _sb_kernel(q_ref, k_ref, v_ref, tri_ref, o_ref, carry_sc, acc_sc, *, tq, nsub):
    i = pl.program_id(1)
    s_id = pl.program_id(2)
    ks = nsub * TILE
    kb = ((i + 1) * tq - 1) // ks - s_id
    rows = H_SB * tq

    @pl.when(s_id == 0)
    def _():
        carry_sc[...] = jnp.zeros(carry_sc.shape, F32)
        acc_sc[...] = jnp.zeros(acc_sc.shape, F32)

    live = jnp.max(carry_sc[...]) > SB_DEAD_LOG

    @pl.when((kb >= 0) & ((s_id == 0) | live))
    def _():
        q = q_ref[...].reshape(rows, HEAD_DIM)
        qpos = i * tq + (lax.broadcasted_iota(jnp.int32, (rows, TILE), 0) & (tq - 1))
        lane = lax.broadcasted_iota(jnp.int32, (rows, TILE), 1)
        for u in reversed(range(nsub)):
            kstart = kb * ks + u * TILE

            @pl.when(kstart < (i + 1) * tq)
            def _():
                z = _dot_nt(q, k_ref[u]) * HEAD_DIM ** -0.5
                before = (kstart + lane) < qpos
                ls = jnp.minimum(z, 0.0) - jnp.log1p(jnp.exp(-jnp.abs(z)))
                lk = jnp.where(before, ls - z, 0.0)
                both = _exact_dot(lk, tri_ref[...])
                later = both[:, :TILE] + carry_sc[...]
                a = jnp.where(before, jnp.exp(ls + later), 0.0)
                acc_sc[...] += _dot(a, v_ref[u])
                carry_sc[...] += both[:, TILE:]

    @pl.when(s_id == pl.num_programs(2) - 1)
    def _():
        _store_heads(o_ref, acc_sc[...], H_SB, tq, HEAD_DIM)


def sb_prompt(q4, k, v):
    b, _, l, hd = q4.shape
    tq = TILE
    nsub = min(4, l // TILE)
    ks = nsub * TILE
    nt = l // ks
    idx = lax.broadcasted_iota(jnp.int32, (TILE, TILE), 0) > lax.broadcasted_iota(jnp.int32, (TILE, TILE), 1)
    tri = jnp.concatenate([idx.astype(MXU_DTYPE), jnp.ones((TILE, TILE), MXU_DTYPE)], axis=1)
    kmap = lambda bb, i, s: (bb, jnp.maximum(((i + 1) * tq - 1) // ks - s, 0), 0, 0)
    k4 = _mx(k).reshape(b, nt * nsub, TILE, hd)
    v4 = _mx(v).reshape(b, nt * nsub, TILE, hd)
    return pl.pallas_call(
        functools.partial(_sb_kernel, tq=tq, nsub=nsub),
        grid=(b, l // tq, nt),
        in_specs=[pl.BlockSpec((None, H_SB, tq, hd), lambda bb, i, s: (bb, 0, i, 0)),
                  pl.BlockSpec((None, nsub, TILE, hd), kmap),
                  pl.BlockSpec((None, nsub, TILE, hd), kmap),
                  pl.BlockSpec((TILE, 2 * TILE), lambda bb, i, s: (0, 0))],
        out_specs=pl.BlockSpec((None, tq, H_SB * hd), lambda bb, i, s: (bb, i, 0)),
        out_shape=jax.ShapeDtypeStruct((b, l, H_SB * hd), F32),
        scratch_shapes=[pltpu.VMEM((H_SB * tq, TILE), F32), pltpu.VMEM((H_SB * tq, hd), F32)],
        compiler_params=_cparams("parallel", "parallel", "arbitrary"),
        name="sb_prompt",
    )(_mx(q4), k4, v4, tri)


def _cmp_kernel(x_ref, w1_ref, w2_ref, g_ref, o_ref):
    y = _dot(jax.nn.gelu(_dot(x_ref[...], w1_ref[...])), w2_ref[...])
    yn = y * lax.rsqrt(jnp.mean(y * y, axis=-1, keepdims=True) + EPS) * g_ref[...]
    o_ref[...] = jnp.where(pl.program_id(1) == 0, yn, y)


def compress(blocks, w1, w2, g):
    b, _, nc, kdim = blocks.shape
    return pl.pallas_call(
        _cmp_kernel,
        grid=(b, 2),
        in_specs=[pl.BlockSpec((None, None, nc, kdim), lambda bb, i: (bb, i, 0, 0)),
                  pl.BlockSpec((None, kdim, CMP_HIDDEN), lambda bb, i: (i, 0, 0)),
                  pl.BlockSpec((None, CMP_HIDDEN, HEAD_DIM), lambda bb, i: (i, 0, 0)),
                  pl.BlockSpec((1, HEAD_DIM), lambda bb, i: (0, 0))],
        out_specs=pl.BlockSpec((None, None, nc, HEAD_DIM), lambda bb, i: (bb, i, 0, 0)),
        out_shape=jax.ShapeDtypeStruct((b, 2, nc, HEAD_DIM), F32),
        compiler_params=_cparams("parallel", "parallel"),
        name="nsa_compress",
    )(blocks, _mx(w1), _mx(w2), g.reshape(1, HEAD_DIM))


def _nsa_kernel(q_ref, gate_ref, kc_ref, vc_ref, bc_ref, ks_ref, vs_ref, kw_ref, vw_ref, smat_ref, tb_ref, o_ref,
                *, n_sel):
    i = pl.program_id(1)
    rows = H_NSA * TILE
    scale = HEAD_DIM ** -0.5
    ncp = kc_ref.shape[0]
    nsp = smat_ref.shape[1]
    q = q_ref[...].reshape(rows, HEAD_DIM)
    qp4 = i * TILE + (lax.broadcasted_iota(jnp.int32, (rows, 1), 0) & (TILE - 1))
    qp = i * TILE + lax.broadcasted_iota(jnp.int32, (TILE, 1), 0)
    lane = lax.broadcasted_iota(jnp.int32, (TILE, TILE), 1)
    rep4 = lambda x: jnp.concatenate([x] * H_NSA, axis=0)

    cmp_end = lax.broadcasted_iota(jnp.int32, (1, ncp), 1) * CMP_STRIDE + (CMP_BLOCK - 1)
    s = _dot_nt(q, kc_ref[...]) * scale + bc_ref[...].reshape(rows, ncp)
    s = jnp.where(qp4 >= cmp_end, s, NEG_INF)
    m = jnp.max(s, axis=1, keepdims=True)
    e = jnp.exp(s - jnp.where(m == NEG_INF, 0.0, m))
    d = jnp.sum(e, axis=1, keepdims=True)
    pc = e / jnp.where(d > 0, d, 1.0)
    o_c = _dot(pc, vc_ref[...])

    imp = ((pc[0:TILE] + pc[TILE:2 * TILE]) + pc[2 * TILE:3 * TILE]) + pc[3 * TILE:4 * TILE]
    score = _exact_dot(imp, smat_ref[...])
    blk = lax.broadcasted_iota(jnp.int32, (TILE, nsp), 1)
    back = qp // SLC_BLOCK - blk
    forced = (blk == 0) | ((back >= 0) & (back < N_LOCAL))
    score = jnp.where(back >= 0, jnp.where(forced, jnp.inf, score), NEG_INF)
    sel_f = jnp.where(_topk_mask_row(score, n_sel), 1.0, 0.0)

    zero_state = (jnp.full((rows, 1), NEG_INF, F32), jnp.zeros((rows, 1), F32), jnp.zeros((rows, HEAD_DIM), F32))
    blk_row = lax.broadcasted_iota(jnp.int32, (nsp, TILE), 0)
    half = lax.broadcasted_iota(jnp.int32, (nsp, TILE), 1) // SLC_BLOCK

    def slc_body(j, st):
        expand = jnp.where(blk_row == j * (TILE // SLC_BLOCK) + half, 1.0, 0.0)
        picked = _dot(sel_f, expand) > 0.5
        ok = picked & ((j * TILE + lane) <= qp)
        sc = _dot_nt(q, ks_ref[j]) * scale + _tile_bias(tb_ref, i - j, H_NSA)
        return _softmax_step(sc, rep4(ok), *st, vs_ref[j])

    _, l_s, acc_s = lax.fori_loop(0, i + 1, slc_body, zero_state)
    o_s = acc_s / jnp.where(l_s > 0, l_s, 1.0)

    def win_body(j, st):
        dist = qp - (j * TILE + lane)
        ok = (dist >= 0) & (dist < WINDOW)
        sc = _dot_nt(q, kw_ref[j]) * scale + _tile_bias(tb_ref, i - j, H_NSA)
        return _softmax_step(sc, rep4(ok), *st, vw_ref[j])

    _, l_w, acc_w = lax.fori_loop(jnp.maximum(i - WINDOW // TILE, 0), i + 1, win_body, zero_state)
    o_w = acc_w / jnp.where(l_w > 0, l_w, 1.0)

    g = gate_ref[...].reshape(rows, 3)
    o = g[:, 0:1] * o_c + g[:, 1:2] * o_s + g[:, 2:3] * o_w
    _store_heads(o_ref, o, H_NSA, TILE, HEAD_DIM)


def nsa_prompt(q4, gate4, cmp_kv, bias_c, k_slc, v_slc, k_win, v_win, smat, tb):
    b, _, l, hd = q4.shape
    nt = l // TILE
    ncp = cmp_kv.shape[2]
    nsp = smat.shape[1]
    n_sel = min(N_SELECT, l // SLC_BLOCK)
    tiles = lambda x: _mx(x).reshape(b, nt, TILE, hd)
    kv_spec = pl.BlockSpec((None, nt, TILE, hd), lambda bb, i: (bb, 0, 0, 0))
    return pl.pallas_call(
        functools.partial(_nsa_kernel, n_sel=n_sel),
        grid=(b, nt),
        in_specs=[pl.BlockSpec((None, H_NSA, TILE, hd), lambda bb, i: (bb, 0, i, 0)),
                  pl.BlockSpec((None, H_NSA, TILE, 3), lambda bb, i: (bb, 0, i, 0)),
                  pl.BlockSpec((None, None, ncp, hd), lambda bb, i: (bb, 0, 0, 0)),
                  pl.BlockSpec((None, None, ncp, hd), lambda bb, i: (bb, 1, 0, 0)),
                  pl.BlockSpec((H_NSA, TILE, ncp), lambda bb, i: (0, i, 0)),
                  kv_spec, kv_spec, kv_spec, kv_spec,
                  pl.BlockSpec((ncp, nsp), lambda bb, i: (0, 0)),
                  pl.BlockSpec((3, H_NSA, TILE, TILE), lambda bb, i: (0, 0, 0, 0))],
        out_specs=pl.BlockSpec((None, TILE, H_NSA * hd), lambda bb, i: (bb, i, 0)),
        out_shape=jax.ShapeDtypeStruct((b, l, H_NSA * hd), F32),
        compiler_params=_cparams("parallel", "parallel"),
        name="nsa_prompt",
    )(_mx(q4), gate4, _mx(cmp_kv), _mx(cmp_kv), bias_c, tiles(k_slc), tiles(v_slc), tiles(k_win), tiles(v_win),
      smat, tb)


def _dsa_kernel(q_ref, qi_ref, w_ref, kd_ref, vd_ref, ki_ref, tb_ref, o_ref, key_sc, *, n_keep):
    i = pl.program_id(1)
    rows = H_DSA * TILE
    scale = HEAD_DIM ** -0.5
    q = q_ref[...].reshape(rows, HEAD_DIM)
    qp = i * TILE + lax.broadcasted_iota(jnp.int32, (TILE, 1), 0)
    lane = lax.broadcasted_iota(jnp.int32, (TILE, TILE), 1)
    w = w_ref[...]

    def index_body(j, c):
        ki = ki_ref[j]
        isc = None
        for h in range(H_IDX):
            t = w[:, h:h + 1] * jnp.maximum(_dot_nt(qi_ref[h], ki), 0.0)
            isc = t if isc is None else isc + t
        isc = jnp.where((j * TILE + lane) <= qp, isc, NEG_INF)
        key_sc[j] = _order_key(isc)
        return c

    lax.fori_loop(0, i + 1, index_body, 0)

    def count_ge(t):
        def body(j, c):
            return c + jnp.where(key_sc[j] >= t, 1.0, 0.0)
        return jnp.sum(lax.fori_loop(0, i + 1, body, jnp.zeros((TILE, TILE), F32)), axis=1, keepdims=True)

    t = _kth_threshold(count_ge, float(n_keep), TILE)
    need = float(n_keep) - count_ge(t + 1)
    tri = _lower_tri(TILE)
    hg = H_DSA // HEAD_GROUPS
    gr = hg * TILE
    rep = lambda x: jnp.concatenate([x] * hg, axis=0)
    for grp in range(HEAD_GROUPS):
        qg = q[grp * gr:(grp + 1) * gr]

        def att_body(j, st, grp=grp, qg=qg):
            m, l, acc, seen = st
            key = key_sc[j]
            eq = key == t
            eq_f = jnp.where(eq, 1.0, 0.0)
            prefix = _dot(eq_f, tri) + seen
            ok = ((key > t) | (eq & (prefix < need))) & (key > KEY_NEG_INF)
            bias = _tile_bias(tb_ref, i - j, H_DSA)[grp * gr:(grp + 1) * gr]
            sc = _dot_nt(qg, kd_ref[j]) * scale + bias
            m, l, acc = _softmax_step(sc, rep(ok), m, l, acc, vd_ref[j])
            return m, l, acc, seen + jnp.sum(eq_f, axis=1, keepdims=True)

        st0 = (jnp.full((gr, 1), NEG_INF, F32), jnp.zeros((gr, 1), F32), jnp.zeros((gr, HEAD_DIM), F32),
               jnp.zeros((TILE, 1), F32))
        _, l, acc, _ = lax.fori_loop(0, i + 1, att_body, st0)
        o = acc / jnp.where(l > 0, l, 1.0)
        for h in range(hg):
            col = (grp * hg + h) * HEAD_DIM
            o_ref[:, col:col + HEAD_DIM] = o[h * TILE:(h + 1) * TILE]


def dsa_prompt(q4, qi4, w_idx, k, v, k_idx, tb):
    b, _, l, hd = q4.shape
    nt = l // TILE
    n_keep = min(TOPK_KEYS, l // 4)
    tiles = lambda x: _mx(x).reshape(b, nt, TILE, x.shape[-1])
    kv_spec = lambda wd: pl.BlockSpec((None, nt, TILE, wd), lambda bb, i: (bb, 0, 0, 0))
    return pl.pallas_call(
        functools.partial(_dsa_kernel, n_keep=n_keep),
        grid=(b, nt),
        in_specs=[pl.BlockSpec((None, H_DSA, TILE, hd), lambda bb, i: (bb, 0, i, 0)),
                  pl.BlockSpec((None, H_IDX, TILE, D_IDX), lambda bb, i: (bb, 0, i, 0)),
                  pl.BlockSpec((None, TILE, H_IDX), lambda bb, i: (bb, i, 0)),
                  kv_spec(hd), kv_spec(hd), kv_spec(D_IDX),
                  pl.BlockSpec((3, H_DSA, TILE, TILE), lambda bb, i: (0, 0, 0, 0))],
        out_specs=pl.BlockSpec((None, TILE, H_DSA * hd), lambda bb, i: (bb, i, 0)),
        out_shape=jax.ShapeDtypeStruct((b, l, H_DSA * hd), F32),
        scratch_shapes=[pltpu.VMEM((nt, TILE, TILE), jnp.int32)],
        compiler_params=_cparams("parallel", "arbitrary"),
        name="dsa_prompt",
    )(_mx(q4), _mx(qi4), w_idx, tiles(k), tiles(v), tiles(k_idx), tb)


PAGES_PER_STEP = 16
QROWS = 8


def _page_specs(width, layer, npp, chunk_of, transposed=False):
    shape = (None, None, width, PAGE_ROWS) if transposed else (None, None, PAGE_ROWS, width)

    def spec(n):
        return pl.BlockSpec(shape, lambda s, c, pt: (layer, pt[s, chunk_of(c) * npp + n], 0, 0))
    return [spec(n) for n in range(npp)]


PAGE_ROWS = 128


def _lanes(x, times):
    return x if times == 1 else jnp.concatenate([x] * times, axis=1)


def _decode_softmax_step(s, ok, m, l):
    s = jnp.where(ok, s, NEG_INF)
    m_new = jnp.maximum(m, jnp.max(s, axis=1, keepdims=True))
    m_safe = jnp.where(m_new == NEG_INF, 0.0, m_new)
    p = jnp.exp(s - m_safe)
    alpha = jnp.exp(m - m_safe)
    return m_new, alpha * l + jnp.sum(p, axis=1, keepdims=True), p, alpha


def _mla_s_kernel(pt_ref, qg_ref, qr_ref, wt_ref, new_ref, *rest, npp, sub):
    pages = rest[:npp]
    o_ref, m_sc, l_sc, acc_sc = rest[npp:]
    c = pl.program_id(1)
    scale = (DN_MLA + DR_MLA) ** -0.5
    hw = DN_MLA + DV_MLA

    @pl.when(c == 0)
    def _():
        m_sc[...] = jnp.full(m_sc.shape, NEG_INF, F32)
        l_sc[...] = jnp.zeros(l_sc.shape, F32)
        acc_sc[...] = jnp.zeros(acc_sc.shape, F32)

    w_ext = jnp.concatenate([wt_ref[...], _mx(qr_ref[...])], axis=0)

    def process(rows_t, ok):
        n = rows_t.shape[1]
        kvt = jnp.dot(w_ext, rows_t, preferred_element_type=F32)
        s_rows = []
        for h in range(H_MLA):
            kk = kvt[h * hw:h * hw + DN_MLA]
            ss = jnp.sum(kk * kk, axis=0, keepdims=True)
            a = jnp.sum(kk * _lanes(qg_ref[h * DN_MLA:(h + 1) * DN_MLA, :], n // LANES), axis=0, keepdims=True)
            s_rows.append(a * lax.rsqrt(ss / DN_MLA + EPS))
        s = jnp.concatenate(s_rows + [jnp.zeros((QROWS - H_MLA, n), F32)], axis=0)
        s = (s + kvt[H_MLA * hw:H_MLA * hw + QROWS]) * scale
        m, l, p, alpha = _decode_softmax_step(s, ok, m_sc[...], l_sc[...])
        m_sc[...] = m
        l_sc[...] = l
        for h in range(H_MLA):
            pv = kvt[h * hw + DN_MLA:(h + 1) * hw] * p[h:h + 1, :]
            part = pv[:, 0:LANES]
            for t in range(1, n // LANES):
                part = part + pv[:, t * LANES:(t + 1) * LANES]
            acc_sc[h] = alpha[h:h + 1, :] * acc_sc[h] + part

    for g in range(npp // sub):
        process(jnp.concatenate([_mx(pg[...]) for pg in pages[g * sub:(g + 1) * sub]], axis=1), True)

    @pl.when(c == pl.num_programs(1) - 1)
    def _():
        process(_mx(new_ref[...]), lax.broadcasted_iota(jnp.int32, (1, LANES), 1) == 0)
        l = l_sc[...]
        for h in range(H_MLA):
            o = jnp.sum(acc_sc[h], axis=1, keepdims=True) / l[h:h + 1, :]
            o_ref[h * DV_MLA:(h + 1) * DV_MLA, :] = jnp.broadcast_to(o, (DV_MLA, LANES))


def mla_sample(cache, layer, page_table, mla_q, new_rows, w_ukv, g_kn):
    s_n, n_pages = page_table.shape
    npp = min(PAGES_PER_STEP, n_pages)
    hw = DN_MLA + DV_MLA
    qg = (mla_q[..., :DN_MLA] * g_kn).reshape(s_n, H_MLA * DN_MLA, 1)
    qg = jnp.broadcast_to(qg, (s_n, H_MLA * DN_MLA, LANES))
    qr = jnp.pad(mla_q[..., DN_MLA:], ((0, 0), (0, QROWS - H_MLA), (KV_LORA, 0)))
    wt = _mx(jnp.pad(w_ukv.T, ((0, 0), (0, DR_MLA))))
    new_pad = jnp.pad(new_rows[:, :, None], ((0, 0), (0, 0), (0, LANES - 1)))
    grid_spec = pltpu.PrefetchScalarGridSpec(
        num_scalar_prefetch=1, grid=(s_n, n_pages // npp),
        in_specs=[pl.BlockSpec((None, H_MLA * DN_MLA, LANES), lambda s, c, pt: (s, 0, 0)),
                  pl.BlockSpec((None, QROWS, MLA_KV_W), lambda s, c, pt: (s, 0, 0)),
                  pl.BlockSpec((H_MLA * hw, MLA_KV_W), lambda s, c, pt: (0, 0)),
                  pl.BlockSpec((None, MLA_KV_W, LANES), lambda s, c, pt: (s, 0, 0))]
                 + _page_specs(MLA_KV_W, layer, npp, lambda c: c, transposed=True),
        out_specs=pl.BlockSpec((None, H_MLA * DV_MLA, LANES), lambda s, c, pt: (s, 0, 0)),
        scratch_shapes=[pltpu.VMEM((QROWS, 1), F32), pltpu.VMEM((QROWS, 1), F32),
                        pltpu.VMEM((H_MLA, DV_MLA, LANES), F32)])
    out = pl.pallas_call(
        functools.partial(_mla_s_kernel, npp=npp, sub=min(4, npp)), grid_spec=grid_spec,
        out_shape=jax.ShapeDtypeStruct((s_n, H_MLA * DV_MLA, LANES), F32),
        compiler_params=_cparams("parallel", "arbitrary"), name="mla_sample",
    )(page_table, qg, qr, wt, new_pad, *([cache] * npp))
    return out[:, :, 0]


def _sb_s_kernel(pt_ref, q_ref, tri_ref, t2_ref, *rest, npp):
    pages = rest[:npp]
    o_ref, carry_sc, acc_sc = rest[npp:]
    c = pl.program_id(1)
    nt = npp * PAGE_ROWS // LANES

    @pl.when(c == 0)
    def _():
        carry_sc[...] = jnp.zeros(carry_sc.shape, F32)
        acc_sc[...] = jnp.zeros(acc_sc.shape, F32)

    @pl.when(jnp.max(carry_sc[...]) > SB_DEAD_LOG)
    def _():
        k = jnp.concatenate([_mx(pg[:, :HEAD_DIM]) for pg in pages], axis=0)
        v = jnp.concatenate([_mx(pg[:, HEAD_DIM:]) for pg in pages], axis=0)
        z8 = _dot_nt(q_ref[...], k) * HEAD_DIM ** -0.5
        z = jnp.concatenate([z8[:, t * LANES:(t + 1) * LANES] for t in range(nt)], axis=0)
        ls = jnp.minimum(z, 0.0) - jnp.log1p(jnp.exp(-jnp.abs(z)))
        both = _exact_dot(ls - z, tri_ref[...])
        tot = both[:, LANES:]
        later_tiles = _exact_dot(tot, t2_ref[...], left=False)
        later = both[:, :LANES] + later_tiles + jnp.concatenate([carry_sc[...]] * nt, axis=0)
        a = jnp.exp(ls + later)
        a8 = jnp.concatenate([a[t * QROWS:(t + 1) * QROWS] for t in range(nt)], axis=1)
        acc_sc[...] += _dot(a8, v)
        carry_sc[...] += later_tiles[0:QROWS] + tot[0:QROWS]

    @pl.when(c == pl.num_programs(1) - 1)
    def _():
        o_ref[...] = acc_sc[...]


def sb_sample(cache, layer, page_table, sb_q):
    s_n, n_pages = page_table.shape
    npp = min(PAGES_PER_STEP, n_pages)
    nc = n_pages // npp
    nt = npp * PAGE_ROWS // LANES
    rows = nt * QROWS
    q8 = jnp.pad(sb_q, ((0, 0), (0, QROWS - H_SB), (0, 0)))
    idx = lax.broadcasted_iota(jnp.int32, (LANES, LANES), 0) > lax.broadcasted_iota(jnp.int32, (LANES, LANES), 1)
    tri = jnp.concatenate([idx.astype(MXU_DTYPE), jnp.ones((LANES, LANES), MXU_DTYPE)], axis=1)
    r = lax.broadcasted_iota(jnp.int32, (rows, rows), 0)
    cidx = lax.broadcasted_iota(jnp.int32, (rows, rows), 1)
    t2 = ((r % QROWS == cidx % QROWS) & (cidx // QROWS > r // QROWS)).astype(MXU_DTYPE)
    grid_spec = pltpu.PrefetchScalarGridSpec(
        num_scalar_prefetch=1, grid=(s_n, nc),
        in_specs=[pl.BlockSpec((None, QROWS, HEAD_DIM), lambda s, c, pt: (s, 0, 0)),
                  pl.BlockSpec((LANES, 2 * LANES), lambda s, c, pt: (0, 0)),
                  pl.BlockSpec((rows, rows), lambda s, c, pt: (0, 0))]
                 + _page_specs(SB_KV_W, layer, npp, lambda c: nc - 1 - c),
        out_specs=pl.BlockSpec((None, QROWS, HEAD_DIM), lambda s, c, pt: (s, 0, 0)),
        scratch_shapes=[pltpu.VMEM((QROWS, LANES), F32), pltpu.VMEM((QROWS, HEAD_DIM), F32)])
    out = pl.pallas_call(
        functools.partial(_sb_s_kernel, npp=npp), grid_spec=grid_spec,
        out_shape=jax.ShapeDtypeStruct((s_n, QROWS, HEAD_DIM), F32),
        compiler_params=_cparams("parallel", "arbitrary"), name="sb_sample",
    )(page_table, _mx(q8), tri, t2, *([cache] * npp))
    return out[:, :H_SB].reshape(s_n, H_SB * HEAD_DIM)


def _first_indices_below(eq, idx, need, nbits):
    def count_lt(t):
        return sum(jnp.sum(jnp.where(e & (i < t), 1.0, 0.0), keepdims=True) for e, i in zip(eq, idx))

    def step(s, t):
        cand = t + (jnp.int32(1) << (nbits - 1 - s))
        return jnp.where(count_lt(cand) <= need, cand, t)

    return lax.fori_loop(0, nbits, step, jnp.zeros((1, 1), jnp.int32))


def _dsa_s_kernel(pt_ref, q_ref, qi_ref, w_ref, bias_ref, new_ref, *rest, npp, n_keep):
    pages = rest[:npp]
    o_ref, key_sc, kv_sc = rest[npp:]
    c = pl.program_id(1)
    nc = pl.num_programs(1)
    hd = HEAD_DIM
    n = npp * PAGE_ROWS
    scale = hd ** -0.5

    def index_keys(kidx_t, width):
        rel = jnp.maximum(_dot(qi_ref[...], kidx_t), 0.0)
        return jnp.sum(_lanes(w_ref[...], width // LANES) * rel, axis=0, keepdims=True)

    kidx_t = jnp.concatenate([_mx(pg[2 * hd:, :]) for pg in pages], axis=1)
    key_sc[pl.ds(c, 1), :] = _order_key(index_keys(kidx_t, n))
    kv_sc[c] = jnp.concatenate([_mx(pg[:2 * hd, :]) for pg in pages], axis=1)

    @pl.when(c == nc - 1)
    def _():
        nck = key_sc.shape[0]
        lane_n = lax.broadcasted_iota(jnp.int32, (1, LANES), 1)
        new = _mx(new_ref[...])
        key_new = _order_key(jnp.where(lane_n == 0, index_keys(new[2 * hd:, :], LANES), NEG_INF))
        keys = key_sc[...]
        idx = lax.broadcasted_iota(jnp.int32, (nck, n), 0) * n + lax.broadcasted_iota(jnp.int32, (nck, n), 1)
        idx_new = nck * n + lane_n

        def count_ge(t):
            return (jnp.sum(jnp.where(keys >= t, 1.0, 0.0), keepdims=True)
                    + jnp.sum(jnp.where(key_new >= t, 1.0, 0.0), keepdims=True))

        t = _kth_threshold(count_ge, float(n_keep), 1)
        need = float(n_keep) - count_ge(t + 1)
        nbits = int(nck * n + LANES).bit_length()
        lim = _first_indices_below([keys == t, key_new == t], [idx, idx_new], need, nbits)
        sel = ((keys > t) | ((keys == t) & (idx < lim))) & (keys > KEY_NEG_INF)
        sel_new = ((key_new > t) | ((key_new == t) & (idx_new < lim))) & (key_new > KEY_NEG_INF)

        q = q_ref[...]
        m = jnp.full((QROWS, 1), NEG_INF, F32)
        l = jnp.zeros((QROWS, 1), F32)
        acc = jnp.zeros((QROWS, hd), F32)
        for cc in range(nck):
            kv = kv_sc[cc]
            s = _dot(q, kv[:hd, :]) * scale + bias_ref[cc]
            m, l, p, alpha = _decode_softmax_step(s, sel[cc:cc + 1, :], m, l)
            acc = alpha * acc + _dot_nt(p, kv[hd:, :])
        s = _dot(q, new[:hd, :]) * scale + bias_ref[nck][:, :LANES]
        m, l, p, alpha = _decode_softmax_step(s, sel_new, m, l)
        acc = alpha * acc + _dot_nt(p, new[hd:2 * hd, :])
        o_ref[...] = acc / jnp.where(l > 0, l, 1.0)


def dsa_sample(cache, layer, page_table, dsa_q, q_idx, w_idx, new_rows, bias_keys):
    s_n, n_pages = page_table.shape
    npp = min(PAGES_PER_STEP, n_pages)
    nc = n_pages // npp
    n = npp * PAGE_ROWS
    total = n_pages * PAGE_ROWS + 1
    pad_h = lambda x: jnp.pad(x, ((0, 0), (0, QROWS - x.shape[1]), (0, 0)))
    w8 = jnp.broadcast_to(pad_h(w_idx[:, :, None]), (s_n, QROWS, LANES))
    new_pad = jnp.pad(new_rows[:, :, None], ((0, 0), (0, 0), (0, LANES - 1)))
    grid_spec = pltpu.PrefetchScalarGridSpec(
        num_scalar_prefetch=1, grid=(s_n, nc),
        in_specs=[pl.BlockSpec((None, QROWS, HEAD_DIM), lambda s, c, pt: (s, 0, 0)),
                  pl.BlockSpec((None, QROWS, D_IDX), lambda s, c, pt: (s, 0, 0)),
                  pl.BlockSpec((None, QROWS, LANES), lambda s, c, pt: (s, 0, 0)),
                  pl.BlockSpec((nc + 1, QROWS, n), lambda s, c, pt: (0, 0, 0)),
                  pl.BlockSpec((None, DSA_KV_W, LANES), lambda s, c, pt: (s, 0, 0))]
                 + _page_specs(DSA_KV_W, layer, npp, lambda c: c, transposed=True),
        out_specs=pl.BlockSpec((None, QROWS, HEAD_DIM), lambda s, c, pt: (s, 0, 0)),
        scratch_shapes=[pltpu.VMEM((nc, n), jnp.int32), pltpu.VMEM((nc, 2 * HEAD_DIM, n), MXU_DTYPE)])
    out = pl.pallas_call(
        functools.partial(_dsa_s_kernel, npp=npp, n_keep=min(TOPK_KEYS, total // 4)), grid_spec=grid_spec,
        out_shape=jax.ShapeDtypeStruct((s_n, QROWS, HEAD_DIM), F32),
        compiler_params=_cparams("parallel", "arbitrary"), name="dsa_sample",
    )(page_table, _mx(pad_h(dsa_q)), _mx(pad_h(q_idx)), w8, bias_keys, new_pad, *([cache] * npp))
    return out[:, :H_DSA].reshape(s_n, H_DSA * HEAD_DIM)


def _pool_kernel(x_ref, pe_ref, w_ref, o_ref):
    pp, page, _ = x_ref.shape
    per_page = page // CMP_STRIDE
    acc = [None, None]
    for t in range(CMP_STRIDE):
        xt = x_ref[:, pl.ds(t, per_page, stride=CMP_STRIDE), :].reshape(pp * per_page, 2 * HEAD_DIM)
        for half in range(2):
            y = _dot(xt + pe_ref[half, t:t + 1, :], w_ref[half, t])
            acc[half] = y if acc[half] is None else acc[half] + y
    o_ref[:, :, :2 * CMP_HIDDEN] = acc[0].reshape(pp, per_page, 2 * CMP_HIDDEN)
    o_ref[:, :, 2 * CMP_HIDDEN:] = acc[1].reshape(pp, per_page, 2 * CMP_HIDDEN)


def pool_compress_hidden(cache, layer, w1, pe):
    depth, n_pool, page, width = cache.shape
    per_page = page // CMP_STRIDE
    w1r = w1.reshape(2, 2, CMP_STRIDE, HEAD_DIM, CMP_HIDDEN)
    wz = jnp.zeros((2, CMP_STRIDE, 2 * HEAD_DIM, 2 * CMP_HIDDEN), F32)
    wz = _mx(wz.at[:, :, :HEAD_DIM, :CMP_HIDDEN].set(w1r[0]).at[:, :, HEAD_DIM:, CMP_HIDDEN:].set(w1r[1]))
    pez = jnp.transpose(pe.reshape(2, 2, CMP_STRIDE, HEAD_DIM), (1, 2, 0, 3)).reshape(2, CMP_STRIDE, 2 * HEAD_DIM)
    pp = next(c for c in (32, 16, 8, 4, 2, 1) if n_pool % c == 0)
    return pl.pallas_call(
        _pool_kernel,
        grid=(n_pool // pp,),
        in_specs=[pl.BlockSpec((None, pp, page, 2 * HEAD_DIM), lambda i: (layer, i, 0, 0)),
                  pl.BlockSpec((2, CMP_STRIDE, 2 * HEAD_DIM), lambda i: (0, 0, 0)),
                  pl.BlockSpec((2, CMP_STRIDE, 2 * HEAD_DIM, 2 * CMP_HIDDEN), lambda i: (0, 0, 0, 0))],
        out_specs=pl.BlockSpec((pp, per_page, 4 * CMP_HIDDEN), lambda i: (i, 0, 0)),
        out_shape=jax.ShapeDtypeStruct((n_pool, per_page, 4 * CMP_HIDDEN), F32),
        compiler_params=_cparams("parallel"), name="nsa_pool_hidden",
    )(cache, pez, wz)


def _nsa_s_kernel(pt_ref, q_ref, gate_ref, w2_ref, gk_ref, bc_ref, smat_ref, fold_ref, exp_ref, bs_ref, win_ref,
                  bw_ref, newn_ref, neww_ref, *rest, npp, past_len, n_sel):
    ab_pages = rest[:npp]
    ab_next = rest[npp]
    pages = rest[npp + 1:2 * npp + 1]
    o_ref, sc_sc, vc_sc, kv_sc, r_sc = rest[2 * npp + 1:]
    c = pl.program_id(1)
    nc = pl.num_programs(1)
    hd = HEAD_DIM
    scale = hd ** -0.5
    q = q_ref[...]
    ncb = npp * PAGE_ROWS // CMP_STRIDE

    ab = jnp.concatenate([a[...] for a in ab_pages], axis=0)
    second = ab[:, 2 * CMP_HIDDEN:]
    row = lax.broadcasted_iota(jnp.int32, (ncb, 1), 0)
    second = jnp.where(row == ncb - 1, ab_next[0:1, 2 * CMP_HIDDEN:], jnp.roll(second, -1, axis=0))
    hid = jax.nn.gelu(ab[:, :2 * CMP_HIDDEN] + second)
    kc = _dot(hid[:, :CMP_HIDDEN], w2_ref[0])
    kc = kc * lax.rsqrt(jnp.mean(kc * kc, axis=-1, keepdims=True) + EPS) * gk_ref[...]
    cmp_end = (c * ncb + lax.broadcasted_iota(jnp.int32, (1, ncb), 1)) * CMP_STRIDE + (CMP_BLOCK - 1)
    sc_sc[c] = jnp.where(cmp_end <= past_len, _dot_nt(q, kc) * scale + bc_ref[c], NEG_INF)
    vc_sc[c] = _dot(hid[:, CMP_HIDDEN:], w2_ref[1])
    kv_sc[c] = jnp.concatenate([_mx(pg[:, 2 * hd:]) for pg in pages], axis=0)

    @pl.when(c == nc - 1)
    def _():
        nck = sc_sc.shape[0]
        n = npp * PAGE_ROWS
        s = jnp.concatenate([sc_sc[cc] for cc in range(nck)], axis=1)
        m = jnp.max(s, axis=1, keepdims=True)
        e = jnp.exp(s - jnp.where(m == NEG_INF, 0.0, m))
        d = jnp.sum(e, axis=1, keepdims=True)
        pc = e / jnp.where(d > 0, d, 1.0)
        o_c = jnp.zeros((QROWS, hd), F32)
        for cc in range(nck):
            o_c = o_c + _dot(pc[:, cc * ncb:(cc + 1) * ncb], vc_sc[cc])
        imp = ((pc[0:1] + pc[1:2]) + pc[2:3]) + pc[3:4]
        score = _exact_dot(jnp.broadcast_to(imp, pc.shape), smat_ref[...])
        nsp = score.shape[1]
        blk = lax.broadcasted_iota(jnp.int32, (QROWS, nsp), 1)
        back = past_len // SLC_BLOCK - blk
        forced = (blk == 0) | ((back >= 0) & (back < N_LOCAL))
        score = jnp.where(back >= 0, jnp.where(forced, jnp.inf, score), NEG_INF)
        sel_f = jnp.where(_topk_mask_row(score, n_sel), 1.0, 0.0)
        bpc = n // SLC_BLOCK
        nrow = r_sc.shape[0]
        chunk_of_blk = lax.broadcasted_iota(jnp.int32, (nrow, nsp), 1) // bpc
        own = chunk_of_blk == lax.broadcasted_iota(jnp.int32, (nrow, nsp), 0)
        sel_rows = jnp.concatenate([sel_f] * (nrow // QROWS), axis=0)
        r_sc[...] = _dot(jnp.where(own, sel_rows, 0.0), fold_ref[...])

        st = (jnp.full((QROWS, 1), NEG_INF, F32), jnp.zeros((QROWS, 1), F32))
        acc = jnp.zeros((QROWS, hd), F32)
        for cc in range(nck):
            picked = _dot(jnp.broadcast_to(r_sc[cc:cc + 1, :], (QROWS, LANES)), exp_ref[...]) > 0.5
            kv = kv_sc[cc]
            sc = _dot_nt(q, kv[:, :hd]) * scale + bs_ref[cc]
            m2, l2, p, alpha = _decode_softmax_step(sc, picked, *st)
            st = (m2, l2)
            acc = alpha * acc + _dot(p, kv[:, hd:])
        lane_n = lax.broadcasted_iota(jnp.int32, (QROWS, LANES), 1)
        newn = _mx(newn_ref[...])
        nb_row, nb_lane = (past_len // SLC_BLOCK) // bpc, (past_len // SLC_BLOCK) % bpc
        own_blk = jnp.sum(jnp.where(lane_n[0:1] == nb_lane, r_sc[nb_row:nb_row + 1, :], 0.0), axis=1, keepdims=True)
        sc = _dot_nt(q, newn[:, 2 * hd:3 * hd]) * scale + bs_ref[nck][:, :LANES]
        m2, l2, p, alpha = _decode_softmax_step(sc, (lane_n == 0) & (own_blk > 0.5), *st)
        acc = alpha * acc + _dot(p, newn[:, 3 * hd:])
        o_s = acc / jnp.where(l2 > 0, l2, 1.0)

        win = _mx(win_ref[...])
        wl = win.shape[0]
        dist = wl - lax.broadcasted_iota(jnp.int32, (QROWS, wl), 1)
        sc = _dot_nt(q, win[:, :hd]) * scale + bw_ref[:, :wl]
        st = (jnp.full((QROWS, 1), NEG_INF, F32), jnp.zeros((QROWS, 1), F32))
        m3, l3, p, alpha = _decode_softmax_step(sc, dist < WINDOW, *st)
        acc = _dot(p, win[:, hd:])
        neww = _mx(neww_ref[...])
        sc = _dot_nt(q, neww[:, :hd]) * scale + bw_ref[:, wl:wl + LANES]
        m3, l3, p, alpha = _decode_softmax_step(sc, lane_n == 0, m3, l3)
        acc = alpha * acc + _dot(p, neww[:, hd:])
        o_w = acc / jnp.where(l3 > 0, l3, 1.0)

        g = gate_ref[...]
        o_ref[...] = g[:, 0:1] * o_c + g[:, 1:2] * o_s + g[:, 2:3] * o_w


def nsa_sample(cache, layer, page_table, hidden_pool, nsa_q, nsa_g, new_nsa, new_win, win_state, w2, g_k0,
               bias_cmp, bias_keys, bias_win, smat):
    s_n, n_pages = page_table.shape
    npp = min(PAGES_PER_STEP, n_pages)
    nc = n_pages // npp
    n = npp * PAGE_ROWS
    past_len = n_pages * PAGE_ROWS
    ncb = n // CMP_STRIDE
    per_page = PAGE_ROWS // CMP_STRIDE
    bpc = n // SLC_BLOCK
    nsp = smat.shape[1]
    nrow = -(-(nsp // bpc + 1) // QROWS) * QROWS
    pad_h = lambda x: jnp.pad(x, ((0, 0), (0, QROWS - x.shape[1]), (0, 0)))
    gate8 = jnp.pad(pad_h(nsa_g), ((0, 0), (0, 0), (0, LANES - 3)))
    pad_row = lambda x: jnp.pad(x[:, None, :], ((0, 0), (0, LANES - 1), (0, 0)))
    fold = (lax.broadcasted_iota(jnp.int32, (nsp, LANES), 0) % bpc
            == lax.broadcasted_iota(jnp.int32, (nsp, LANES), 1)).astype(MXU_DTYPE)
    expand = (lax.broadcasted_iota(jnp.int32, (LANES, n), 0)
              == lax.broadcasted_iota(jnp.int32, (LANES, n), 1) // SLC_BLOCK).astype(MXU_DTYPE)
    wl = win_state.shape[1]
    const = lambda shape: pl.BlockSpec(shape, lambda s, c, pt: (0,) * len(shape))
    ab_spec = lambda n_: pl.BlockSpec((None, per_page, 4 * CMP_HIDDEN), lambda s, c, pt: (pt[s, c * npp + n_], 0, 0))
    grid_spec = pltpu.PrefetchScalarGridSpec(
        num_scalar_prefetch=1, grid=(s_n, nc),
        in_specs=[pl.BlockSpec((None, QROWS, HEAD_DIM), lambda s, c, pt: (s, 0, 0)),
                  pl.BlockSpec((None, QROWS, LANES), lambda s, c, pt: (s, 0, 0)),
                  const((2, CMP_HIDDEN, HEAD_DIM)), const((1, HEAD_DIM)),
                  const((nc, QROWS, ncb)), const(smat.shape), const((nsp, LANES)), const((LANES, n)),
                  const((nc + 1, QROWS, n)),
                  pl.BlockSpec((None, wl, WIN_KV_W), lambda s, c, pt: (s, 0, 0)),
                  const((QROWS, wl + LANES)),
                  pl.BlockSpec((None, LANES, NSA_KV_W), lambda s, c, pt: (s, 0, 0)),
                  pl.BlockSpec((None, LANES, WIN_KV_W), lambda s, c, pt: (s, 0, 0))]
                 + [ab_spec(n_) for n_ in range(npp)]
                 + [pl.BlockSpec((None, per_page, 4 * CMP_HIDDEN),
                                 lambda s, c, pt: (pt[s, jnp.minimum((c + 1) * npp, n_pages - 1)], 0, 0))]
                 + _page_specs(NSA_KV_W, layer, npp, lambda c: c),
        out_specs=pl.BlockSpec((None, QROWS, HEAD_DIM), lambda s, c, pt: (s, 0, 0)),
        scratch_shapes=[pltpu.VMEM((nc, QROWS, ncb), F32), pltpu.VMEM((nc, ncb, HEAD_DIM), F32),
                        pltpu.VMEM((nc, n, 2 * HEAD_DIM), MXU_DTYPE), pltpu.VMEM((nrow, LANES), F32)])
    out = pl.pallas_call(
        functools.partial(_nsa_s_kernel, npp=npp, past_len=past_len,
                          n_sel=min(N_SELECT, -(-(past_len + 1) // SLC_BLOCK))),
        grid_spec=grid_spec,
        out_shape=jax.ShapeDtypeStruct((s_n, QROWS, HEAD_DIM), F32),
        compiler_params=_cparams("parallel", "arbitrary"), name="nsa_sample",
    )(page_table, _mx(pad_h(nsa_q)), gate8, _mx(w2), g_k0.reshape(1, HEAD_DIM), bias_cmp, smat, fold, expand,
      bias_keys, win_state, bias_win, pad_row(new_nsa), pad_row(new_win),
      *([hidden_pool] * (npp + 1)), *([cache] * npp))
    return out[:, :H_NSA].reshape(s_n, H_NSA * HEAD_DIM)


def rmsnorm(x, g):
    xf = x.astype(jnp.float32)
    y = xf * lax.rsqrt(jnp.mean(xf * xf, axis=-1, keepdims=True) + EPS)
    return y.astype(x.dtype) * g


def rope(x, pos):
    half = x.shape[-1] // 2
    freqs = ROPE_THETA ** (-jnp.arange(half, dtype=jnp.float32) / half)
    ang = pos.astype(jnp.float32)[..., None] * freqs
    cos, sin = jnp.cos(ang), jnp.sin(ang)
    xf = x.astype(jnp.float32)
    x1, x2 = xf[..., :half], xf[..., half:]
    return jnp.concatenate([x1 * cos - x2 * sin, x2 * cos + x1 * sin], axis=-1).astype(x.dtype)


def t5_bias(table, dist):
    n = jnp.maximum(dist, 0)
    exact = NUM_BUCKETS // 2
    big = exact + (jnp.log(jnp.maximum(n, exact).astype(jnp.float32) / exact)
                   / math.log(T5_MAX_DIST / exact) * (NUM_BUCKETS - exact)).astype(jnp.int32)
    bucket = jnp.where(n < exact, n, jnp.minimum(big, NUM_BUCKETS - 1))
    onehot = (bucket[..., None] == jnp.arange(NUM_BUCKETS, dtype=jnp.int32)).astype(jnp.float32)
    looked_up = jnp.einsum('...b,bh->...h', onehot, table.astype(jnp.float32), precision=lax.Precision.HIGHEST)
    return jnp.moveaxis(looked_up, -1, 0)


def group_transforms(y, pos, p):
    B, L, _ = y.shape
    hd = HEAD_DIM
    seg = lambda c, w: y[..., c:c + w]
    mqa, mkv = seg(C_MQA, Q_LORA), seg(C_MKV, MLA_KV_W)
    nkv, dkv = seg(C_NKV, NSA_KV_W + WIN_KV_W), seg(C_DKV, 2 * hd)
    q = lin(mqa.reshape(B * L, Q_LORA), p['w_mla_uq'], g=p['g_mla_qa'], pro="rms",
            tm=min(512, B * L)).reshape(B, L, H_MLA, DN_MLA + DR_MLA)
    gq = p['g_mla_q']
    mla_q = jnp.concatenate([rmsnorm(q[..., :DN_MLA], gq[:DN_MLA]),
                             rope(rmsnorm(q[..., DN_MLA:], gq[DN_MLA:]), pos[:, None])], axis=-1)
    mla_rows = jnp.concatenate([rmsnorm(mkv[..., :KV_LORA], p['g_mla_kva']),
                                rope(rmsnorm(mkv[..., KV_LORA:], p['g_mla_k'][DN_MLA:]), pos)], axis=-1)
    sb_q = seg(C_SQ, H_SB * hd).reshape(B, L, H_SB, hd)
    nsa_q = rmsnorm(seg(C_NQ, H_NSA * hd).reshape(B, L, H_NSA, hd), p['g_nsa_q'])
    gk = p['g_nsa_k']
    nsa_rows = jnp.concatenate([nkv[..., :2 * hd], rmsnorm(nkv[..., 2 * hd:3 * hd], gk[1]), nkv[..., 3 * hd:4 * hd]], axis=-1)
    win_rows = jnp.concatenate([rmsnorm(nkv[..., 4 * hd:5 * hd], gk[2]), nkv[..., 5 * hd:]], axis=-1)
    nsa_g = jax.nn.sigmoid(seg(C_NG, 3 * H_NSA).reshape(B, L, H_NSA, 3))
    dsa_q = rmsnorm(seg(C_DQ, H_DSA * hd).reshape(B, L, H_DSA, hd), p['g_dsa_q'])
    dsa_rows = jnp.concatenate([rmsnorm(dkv[..., :hd], p['g_dsa_k']), dkv[..., hd:], seg(C_DIK, D_IDX)], axis=-1)
    q_idx = seg(C_DIQ, H_IDX * D_IDX).reshape(B, L, H_IDX, D_IDX)
    qs = (mla_q, sb_q, nsa_q, nsa_g, dsa_q, q_idx, seg(C_DIW, H_IDX))
    rows = (mla_rows, seg(C_SKV, SB_KV_W), nsa_rows, win_rows, dsa_rows)
    return qs, rows


def bias_tiles(table):
    r = jnp.arange(TILE, dtype=jnp.int32)
    tiles = [t5_bias(table, d * TILE + r[:, None] - r[None, :]) for d in (0, 1)]
    far = t5_bias(table, jnp.full((TILE, TILE), 2 * TILE + T5_MAX_DIST, jnp.int32))
    return jnp.stack(tiles + [far], axis=0)


def selection_matrix(ncp, n_cmp, n_slc, nsp):
    ratio = SLC_BLOCK // CMP_STRIDE
    span = CMP_BLOCK // CMP_STRIDE
    s = np.zeros((ncp, nsp), np.float32)
    for b in range(n_slc):
        for m in range(ratio):
            for n in range(span):
                c = ratio * b + m + n - (span - 1)
                if 0 <= c < n_cmp:
                    s[c, b] += 1.0
    return jnp.asarray(s, MXU_DTYPE)


def prompt_mixers(qs, rows, p, tb_nsa, tb_dsa, bias_c, smat):
    mla_q, sb_q, nsa_q, nsa_g, dsa_q, q_idx, w_idx = qs
    mla_r, sb_r, nsa_r, win_r, dsa_r = rows
    B, L = mla_r.shape[:2]
    hd = HEAD_DIM
    heads_first = lambda x: jnp.transpose(x, (0, 2, 1, 3))
    kv = lin(mla_r[..., :KV_LORA].reshape(B * L, KV_LORA), p['w_mla_ukv'], tm=min(512, B * L))
    kv = kv.reshape(B, L, H_MLA, DN_MLA + DV_MLA)
    k_n = rmsnorm(kv[..., :DN_MLA], p['g_mla_k'][:DN_MLA])
    k_r = jnp.broadcast_to(mla_r[:, :, None, KV_LORA:], (B, L, H_MLA, DR_MLA))
    zpad = jnp.zeros((B, L, H_MLA, LANES - DN_MLA - DR_MLA), F32)
    k_pack = _mx(jnp.concatenate([k_n, k_r, zpad], axis=-1)).reshape(B, L, H_MLA * LANES)
    q_pack = _mx(jnp.concatenate([mla_q, zpad], axis=-1)).reshape(B, L, H_MLA * LANES)
    v_pack = _mx(kv[..., DN_MLA:]).reshape(B, L, H_MLA * DV_MLA)
    o_mla = mla_prompt(q_pack, k_pack, v_pack)
    o_sb = sb_prompt(heads_first(sb_q), sb_r[..., :hd], sb_r[..., hd:])
    n_chunk = L // CMP_STRIDE
    n_cmp = n_chunk - 1
    ncp = smat.shape[0]
    raw = jnp.stack([nsa_r[..., :hd], nsa_r[..., hd:2 * hd]], axis=1).reshape(B, 2, n_chunk, CMP_STRIDE * hd)
    blocks = jnp.concatenate([raw[:, :, :-1], raw[:, :, 1:]], axis=-1) + p['pe_cmp'].reshape(1, 2, 1, CMP_BLOCK * hd)
    blocks = jnp.pad(blocks, ((0, 0), (0, 0), (0, ncp - n_cmp), (0, 0)))
    cmp_kv = compress(blocks, p['w_cmp1'], p['w_cmp2'], p['g_nsa_k'][0])
    o_nsa = nsa_prompt(heads_first(nsa_q), heads_first(nsa_g), cmp_kv, bias_c,
                       nsa_r[..., 2 * hd:3 * hd], nsa_r[..., 3 * hd:], win_r[..., :hd], win_r[..., hd:], smat, tb_nsa)
    o_dsa = dsa_prompt(heads_first(dsa_q), heads_first(q_idx), w_idx, dsa_r[..., :hd], dsa_r[..., hd:2 * hd],
                       dsa_r[..., 2 * hd:], tb_dsa)
    return tuple(o.reshape(B * L, BR_WIDTH) for o in (o_mla, o_sb, o_nsa, o_dsa))


def pack_w_in(w):
    cuts = np.cumsum((0,) + IN_SIZES)
    cols = []
    for i in range(10):
        cols.append(jnp.pad(w[:, cuts[i]:cuts[i + 1]], ((0, 0), (0, _SEG_PAD[i] - IN_SIZES[i]))))
    tail = w[:, cuts[10]:cuts[12]]
    cols.append(jnp.pad(tail, ((0, 0), (0, N_SMALL - C_DIK - tail.shape[1]))))
    cols.append(w[:, cuts[12]:])
    return _mx(jnp.concatenate(cols, axis=1))


def kernel(x_prompt, x_sample, cache_mla, cache_sb, cache_nsa, cache_dsa, state_nsa_win, page_table,
           c_prompt, c_sample, w_ada, b_ada, g_norm_mix, g_norm_ffn, w_in, g_mla_qa, w_mla_uq, g_mla_kva,
           w_mla_ukv, g_mla_q, g_mla_k, w_cmp1, w_cmp2, pe_cmp, g_nsa_q, g_nsa_k, g_dsa_q, g_dsa_k, rel_bias,
           w_br, w_out, w_router, b_router, w_gu, b_gu, w_down, b_down):
    B, L, D = x_prompt.shape
    S, seq_s, _ = x_sample.shape
    past_len = page_table.shape[1] * cache_mla.shape[2]
    w_len = state_nsa_win.shape[2]
    w_len_p = min(WINDOW, L)
    pos_p = jnp.arange(L, dtype=jnp.int32)
    pos_s = past_len + jnp.arange(seq_s, dtype=jnp.int32)
    tm_p = min(256, L)
    tpg_p = L // tm_p
    ts = S * seq_s

    tb_nsa = bias_tiles(rel_bias[:, :H_NSA])
    tb_dsa = bias_tiles(rel_bias[:, H_NSA:])
    n_cmp = L // CMP_STRIDE - 1
    ncp = -(-n_cmp // LANES) * LANES
    n_slc = -(-L // SLC_BLOCK)
    nsp = -(-n_slc // LANES) * LANES
    cmp_end = jnp.arange(ncp, dtype=jnp.int32) * CMP_STRIDE + (CMP_BLOCK - 1)
    bias_c = t5_bias(rel_bias[:, :H_NSA], pos_p[:, None] - cmp_end[None, :])
    smat = selection_matrix(ncp, n_cmp, n_slc, nsp)

    assert seq_s == 1
    n_pages = page_table.shape[1]
    npp = min(PAGES_PER_STEP, n_pages)
    nc_s = n_pages // npp
    n_s = npp * PAGE_ROWS
    pad_heads = lambda b: jnp.pad(b, [(0, 0)] * (b.ndim - 2) + [(0, QROWS - b.shape[-2]), (0, 0)])
    kpos = jnp.arange((nc_s + 1) * n_s, dtype=jnp.int32).reshape(nc_s + 1, n_s)
    by_key = lambda tab: pad_heads(jnp.transpose(t5_bias(tab, past_len - kpos), (1, 0, 2)))
    sb_nsa = by_key(rel_bias[:, :H_NSA])
    sb_dsa = by_key(rel_bias[:, H_NSA:])
    ncb_s = n_s // CMP_STRIDE
    cmp_end_s = jnp.arange(nc_s * ncb_s, dtype=jnp.int32).reshape(nc_s, ncb_s) * CMP_STRIDE + (CMP_BLOCK - 1)
    sb_cmp = pad_heads(jnp.transpose(t5_bias(rel_bias[:, :H_NSA], past_len - cmp_end_s), (1, 0, 2)))
    wdist = jnp.concatenate([w_len - jnp.arange(w_len, dtype=jnp.int32), -jnp.arange(LANES, dtype=jnp.int32)])
    sb_win = pad_heads(t5_bias(rel_bias[:, :H_NSA], wdist))
    n_slc_s = -(-(past_len + 1) // SLC_BLOCK)
    smat_s = selection_matrix(nc_s * ncb_s, (past_len + 1) // CMP_STRIDE - 1, n_slc_s, -(-n_slc_s // LANES) * LANES)

    src_col = jnp.arange(2 * D_FF, dtype=jnp.int32)
    src_col = jnp.where(src_col < D_FF, 2 * src_col, 2 * (src_col - D_FF) + 1)
    deinterleave = (jnp.arange(2 * D_FF, dtype=jnp.int32)[:, None] == src_col[None, :]).astype(MXU_DTYPE)

    cache_mla_t = jnp.swapaxes(cache_mla, 2, 3)
    cache_dsa_t = jnp.swapaxes(cache_dsa, 2, 3)

    xp = x_prompt.reshape(B * L, D)
    xs = x_sample.reshape(ts, D)
    c_all = jnp.concatenate([c_prompt, c_sample], axis=0)
    out_p = []
    out_s = []
    for l in range(DEPTH):
        p = {'g_mla_qa': g_mla_qa[l], 'w_mla_uq': w_mla_uq[l], 'g_mla_kva': g_mla_kva[l],
             'w_mla_ukv': w_mla_ukv[l], 'g_mla_q': g_mla_q[l], 'g_mla_k': g_mla_k[l], 'w_cmp1': w_cmp1[l],
             'w_cmp2': w_cmp2[l], 'pe_cmp': pe_cmp[l], 'g_nsa_q': g_nsa_q[l], 'g_nsa_k': g_nsa_k[l],
             'g_dsa_q': g_dsa_q[l], 'g_dsa_k': g_dsa_k[l]}
        mod = lin(c_all, w_ada[l], b=b_ada[l], pro="silu", tn=6 * D // 4)
        mod_p = [m.reshape(B, 1, D) for m in jnp.split(mod[:B], 6, axis=-1)]
        mod_s = [m.reshape(1, ts, D) for m in jnp.split(mod[B:], 6, axis=-1)]
        w_p = pack_w_in(w_in[l])

        y_p, gates_p = proj_in(xp, g_norm_mix[l], mod_p[0], mod_p[1], w_p, tm_p, tpg_p)
        qs_p, rows_p = group_transforms(y_p.reshape(B, L, N_SMALL), pos_p, p)
        o4_p = prompt_mixers(qs_p, rows_p, p, tb_nsa, tb_dsa, bias_c, smat)
        xp = merge(o4_p, gates_p, w_br[l], w_out[l], xp, mod_p[2], tm_p, tpg_p)

        y_s, gates_s = proj_in(xs, g_norm_mix[l], mod_s[0], mod_s[1], w_p, ts, 1)
        qs_s, rows_s = group_transforms(y_s.reshape(S, seq_s, N_SMALL), pos_s, p)

        sq = [a[:, 0] for a in qs_s]
        sr = [a[:, 0] for a in rows_s]
        hidden_pool = pool_compress_hidden(cache_nsa, l, w_cmp1[l], pe_cmp[l])
        o4_s = (mla_sample(cache_mla_t, l, page_table, sq[0], sr[0], w_mla_ukv[l], g_mla_k[l][:DN_MLA]),
                sb_sample(cache_sb, l, page_table, sq[1]),
                nsa_sample(cache_nsa, l, page_table, hidden_pool, sq[2], sq[3], sr[2], sr[3], state_nsa_win[l],
                           w_cmp2[l], g_nsa_k[l][0], sb_cmp, sb_nsa, sb_win, smat_s),
                dsa_sample(cache_dsa_t, l, page_table, sq[4], sq[5], sq[6], sr[4], sb_dsa))
        xs = merge(o4_s, gates_s, w_br[l], w_out[l], xs, mod_s[2], ts, 1)

        h_p, lg_p = router(xp, g_norm_ffn[l], mod_p[3], mod_p[4], w_router[l], b_router[l], tm_p, tpg_p)
        h_s, lg_s = router(xs, g_norm_ffn[l], mod_s[3], mod_s[4], w_router[l], b_router[l], ts, 1)
        wg = lin(w_gu[l].reshape(N_EXPERTS * D, 2 * D_FF), deinterleave, tm=1024,
                 out_dtype=MXU_DTYPE).reshape(N_EXPERTS, D, 2 * D_FF)
        bg = jnp.concatenate([b_gu[l][:, 0::2], b_gu[l][:, 1::2]], axis=-1)[:, None, :]
        y_all = moe_ffn(jnp.concatenate([h_p, h_s], axis=0),
                        jnp.concatenate([lg_p, lg_s], axis=0)[:, :N_EXPERTS],
                        wg, bg, _mx(w_down[l]), b_down[l][:, None, :])
        xp = xp + jnp.broadcast_to(mod_p[5], (B, L, D)).reshape(B * L, D) * y_all[:B * L]
        xs = xs + mod_s[5].reshape(ts, D) * y_all[B * L:]

        out_p.append((rows_p[0], rows_p[1], rows_p[2], rows_p[4], rows_p[3][:, L - w_len_p:]))
        win_s = jnp.concatenate([state_nsa_win[l], rows_s[3]], axis=1)
        out_s.append((rows_s[0], rows_s[1], rows_s[2], rows_s[4], win_s[:, win_s.shape[1] - w_len:]))
    new_p = [jnp.stack([o[i] for o in out_p], axis=0) for i in range(5)]
    new_s = [jnp.stack([o[i] for o in out_s], axis=0) for i in range(5)]
    return (xp.reshape(B, L, D), xs.reshape(S, seq_s, D), *new_p, *new_s)
```
